```python
import jax, jax.numpy as jnp
from jax import lax
import numpy as np

D_MODEL = 1024
BATCH = 16
SEQ = 2048
DEPTH = 1

RET_HEADS = 4
RET_DK = 256
RET_DV = 512
RET_QK_W = RET_HEADS * RET_DK
RET_V_W = RET_HEADS * RET_DV
RET_CHUNK = 128
ROPE_BASE = 10000.0
POOL_WINDOWS = (2, 4, 8, 16)
POOL_GROUPS = 4
POOL_GROUP_W = 256
POOL_W = POOL_GROUPS * POOL_GROUP_W
IN_SIZES = (RET_QK_W, RET_QK_W, RET_V_W, RET_V_W, POOL_W, D_MODEL, D_MODEL)
IN_W = sum(IN_SIZES)
D_FF = 2816
N_MOD = 9
EPS = 1e-6

kernel_name = "hybrid_retention_pool_macaron_adaln"


def rms_norm(x, w):
    xf = x.astype(jnp.float32)
    y = xf * lax.rsqrt(jnp.mean(xf * xf, axis=-1, keepdims=True) + EPS)
    return (y * w.astype(jnp.float32)).astype(x.dtype)


def modulate(h, shift, scale):
    return h * (1 + scale[:, None, :]) + shift[:, None, :]


def swiglu(h, w13, w2):
    a, b = jnp.split(h @ w13, 2, axis=-1)
    return (jax.nn.silu(a) * b) @ w2


def rope(x):
    S, D = x.shape[1], x.shape[-1]
    half = D // 2
    inv = 1.0 / (ROPE_BASE ** (jnp.arange(half, dtype=jnp.float32) / half))
    ang = jnp.arange(S, dtype=jnp.float32)[:, None] * inv[None, :]
    cos = jnp.cos(ang)[None, :, None, :]
    sin = jnp.sin(ang)[None, :, None, :]
    x1, x2 = x[..., :half], x[..., half:]
    return jnp.concatenate([x1 * cos - x2 * sin, x1 * sin + x2 * cos], axis=-1)


def retention_chunkwise(q, k, v):
    B, S, H, DK = q.shape
    DV = v.shape[-1]
    C = RET_CHUNK
    N = S // C
    log_gamma = jnp.log1p(-(2.0 ** (-5.0 - jnp.arange(H, dtype=jnp.float32))))
    idx = jnp.arange(C, dtype=jnp.float32)
    diff = idx[:, None] - idx[None, :]
    inner_decay = jnp.where(diff >= 0, jnp.exp(log_gamma[:, None, None] * jnp.maximum(diff, 0.0)), 0.0)
    q_decay = jnp.exp(log_gamma[:, None] * (idx + 1.0))[None, :, :, None]
    k_decay = jnp.exp(log_gamma[:, None] * (C - 1.0 - idx))[None, :, :, None]
    chunk_decay = jnp.exp(log_gamma * C)[None, :, None, None]

    def to_chunks(t):
        return t.reshape(B, N, C, H, t.shape[-1]).transpose(1, 0, 3, 2, 4)

    qc, kc, vc = to_chunks(q), to_chunks(k), to_chunks(v)

    def step(state, xs):
        qi, ki, vi = xs
        scores = jnp.einsum('bhid,bhjd->bhij', qi, ki) * inner_decay
        inner = jnp.einsum('bhij,bhje->bhie', scores, vi)
        cross = jnp.einsum('bhid,bhde->bhie', qi * q_decay, state)
        new_state = chunk_decay * state + jnp.einsum('bhjd,bhje->bhde', ki * k_decay, vi)
        return new_state, inner + cross

    state0 = jnp.zeros((B, H, DK, DV), jnp.float32)
    _, out = lax.scan(step, state0, (qc, kc, vc))
    return out.transpose(1, 0, 3, 2, 4).reshape(B, S, H, DV)


def head_group_norm(y, w):
    B, S, H, DV = y.shape
    mu = jnp.mean(y, axis=-1, keepdims=True)
    yc = y - mu
    var = jnp.mean(yc * yc, axis=-1, keepdims=True)
    return (yc * lax.rsqrt(var + EPS)).reshape(B, S, H * DV) * w.astype(jnp.float32)


def causal_multiscale_pool(u, lin_w, scale):
    B, S, _ = u.shape
    uf = u.astype(jnp.float32).reshape(B, S, POOL_GROUPS, POOL_GROUP_W)
    cs = jnp.cumsum(uf, axis=1)
    count = jnp.arange(1, S + 1, dtype=jnp.float32)
    outs = []
    for g, w in enumerate(POOL_WINDOWS):
        csg = cs[:, :, g]
        lagged = jnp.pad(csg[:, :S - w], ((0, 0), (w, 0), (0, 0)))
        mean = (csg - lagged) / jnp.minimum(count, float(w))[None, :, None]
        outs.append(mean - uf[:, :, g])
    pooled = jnp.stack(outs, axis=2)
    mixed = jnp.einsum('bsgc,gcd->bsgd', pooled, lin_w.astype(jnp.float32))
    return (mixed.reshape(B, S, POOL_W) * scale.astype(jnp.float32)).astype(u.dtype)


def setup_inputs(seed: int = 0) -> dict:
    key = jax.random.key(seed)
    ks = jax.random.split(key, 20)
    f = jnp.float32

    def nrm(k, shape, s):
        return jax.random.normal(k, shape, f) * s

    L, D = DEPTH, D_MODEL
    return {
        "x": nrm(ks[0], (BATCH, SEQ, D), 1.0),
        "c": nrm(ks[1], (BATCH, D), 1.0),
        "ada_w": nrm(ks[2], (L, D, N_MOD * D), 0.5 * D ** -0.5),
        "ada_b": nrm(ks[3], (L, N_MOD * D), 0.01),
        "norm_ffn1": 1.0 + nrm(ks[4], (L, D), 0.02),
        "ffn1_w13": nrm(ks[5], (L, D, 2 * D_FF), D ** -0.5),
        "ffn1_w2": nrm(ks[6], (L, D_FF, D), D_FF ** -0.5),
        "norm_mix": 1.0 + nrm(ks[7], (L, D), 0.02),
        "w_in": nrm(ks[8], (L, D, IN_W), D ** -0.5),
        "ret_gn_w": 1.0 + nrm(ks[9], (L, RET_V_W), 0.02),
        "w_ret_branch": nrm(ks[10], (L, RET_V_W, D), RET_V_W ** -0.5),
        "pool_lin": nrm(ks[11], (L, POOL_GROUPS, POOL_GROUP_W, POOL_GROUP_W), POOL_GROUP_W ** -0.5),
        "pool_scale": 1.0 + nrm(ks[12], (L, POOL_W), 0.1),
        "w_pool_branch": nrm(ks[13], (L, POOL_W, D), POOL_W ** -0.5),
        "w_out": nrm(ks[14], (L, D, D), D ** -0.5),
        "norm_ffn2": 1.0 + nrm(ks[15], (L, D), 0.02),
        "ffn2_w13": nrm(ks[16], (L, D, 2 * D_FF), D ** -0.5),
        "ffn2_w2": nrm(ks[17], (L, D_FF, D), D_FF ** -0.5),
        "norm_final": 1.0 + nrm(ks[18], (D,), 0.02),
    }


def reference(x, c, ada_w, ada_b, norm_ffn1, ffn1_w13, ffn1_w2, norm_mix, w_in, ret_gn_w,
              w_ret_branch, pool_lin, pool_scale, w_pool_branch, w_out, norm_ffn2, ffn2_w13,
              ffn2_w2, norm_final):
    B, S, D = x.shape
    c_act = jax.nn.silu(c)
    split_pts = np.cumsum(IN_SIZES)[:-1].tolist()
    for l in range(DEPTH):
        mod = c_act @ ada_w[l] + ada_b[l]
        (sh1, sc1, g1, sh2, sc2, g2, sh3, sc3, g3) = jnp.split(mod, N_MOD, axis=-1)

        h = modulate(rms_norm(x, norm_ffn1[l]), sh1, sc1)
        x = x + g1[:, None, :] * (0.5 * swiglu(h, ffn1_w13[l], ffn1_w2[l]))

        h = modulate(rms_norm(x, norm_mix[l]), sh2, sc2)
        proj = h @ w_in[l]
        q, k, v, g, u, a_r, a_p = jnp.split(proj, split_pts, axis=-1)

        qh = rope(q.reshape(B, S, RET_HEADS, RET_DK).astype(jnp.float32))
        kh = rope(k.reshape(B, S, RET_HEADS, RET_DK).astype(jnp.float32)) * (RET_DK ** -0.5)
        vh = v.reshape(B, S, RET_HEADS, RET_DV).astype(jnp.float32)
        ret = head_group_norm(retention_chunkwise(qh, kh, vh), ret_gn_w[l]).astype(x.dtype)
        y_r = (jax.nn.silu(g) * ret) @ w_ret_branch[l]

        y_p = causal_multiscale_pool(u, pool_lin[l], pool_scale[l]) @ w_pool_branch[l]

        merged = jax.nn.sigmoid(a_r) * y_r + jax.nn.sigmoid(a_p) * y_p
        x = x + g2[:, None, :] * (merged @ w_out[l])

        h = modulate(rms_norm(x, norm_ffn2[l]), sh3, sc3)
        x = x + g3[:, None, :] * (0.5 * swiglu(h, ffn2_w13[l], ffn2_w2[l]))
    return rms_norm(x, norm_final)
```

```python
import functools

import jax
import jax.numpy as jnp
from jax import lax
from jax.experimental import pallas as pl
from jax.experimental.pallas import tpu as pltpu

F32 = jnp.float32
BF16 = jnp.bfloat16

D_MODEL = 1024
N_MOD = 9
EPS = 1e-6
D_FF = 2816

RET_HEADS = 4
RET_DK = 256
RET_DV = 512
RET_CHUNK = 128
ROPE_BASE = 10000.0
ROPE_HALF = RET_DK // 2

POOL_WINDOWS = (2, 4, 8, 16)
POOL_GROUP_W = 256
POOL_W = len(POOL_WINDOWS) * POOL_GROUP_W
POOL_HALO = 16

OFF_Q = 0
OFF_K = OFF_Q + RET_HEADS * RET_DK
OFF_V = OFF_K + RET_HEADS * RET_DK
OFF_G = OFF_V + RET_HEADS * RET_DV
OFF_U = OFF_G + RET_HEADS * RET_DV
OFF_AR = OFF_U + POOL_W
OFF_AP = OFF_AR + D_MODEL
IN_W = OFF_AP + D_MODEL

V7X_VMEM_LIMIT_BYTES = 58 * 1024 * 1024
V7X_MXU_WIDTH = 256

FFN_TOKENS = 512
FFN_BLOCK = V7X_MXU_WIDTH
MIX_TOKENS = 256
MOD_BLOCK = 1024


def _dot(a, b):
    return jnp.dot(a, b, preferred_element_type=F32)


def _silu(a):
    return a * jax.nn.sigmoid(a)


def _rms_mod(x, norm_w, shift, scale):
    ms = jnp.mean(x * x, axis=-1, keepdims=True)
    y = x * lax.rsqrt(ms + EPS) * norm_w
    return y * (1.0 + scale) + shift


def _resident(shape):
    zeros = (0,) * len(shape)
    return pl.BlockSpec(shape, lambda *_: zeros, pipeline_mode=pl.Buffered(1))


def _mod_kernel(c_ref, w_ref, b_ref, o_ref):
    c = c_ref[...]
    o_ref[...] = _dot(_silu(c).astype(BF16), w_ref[...].astype(BF16)) + b_ref[...]


def _modulation(c, w, b):
    batch, d = c.shape
    n = w.shape[1]
    return pl.pallas_call(
        _mod_kernel,
        grid=(n // MOD_BLOCK,),
        in_specs=[
            pl.BlockSpec((batch, d), lambda j: (0, 0)),
            pl.BlockSpec((d, MOD_BLOCK), lambda j: (0, j)),
            pl.BlockSpec((1, MOD_BLOCK), lambda j: (0, j)),
        ],
        out_specs=pl.BlockSpec((batch, MOD_BLOCK), lambda j: (0, j)),
        out_shape=jax.ShapeDtypeStruct((batch, n), F32),
        compiler_params=pltpu.CompilerParams(dimension_semantics=("arbitrary",)),
    )(c, w, b.reshape(1, n))


def _ffn_kernel(x_ref, mod_ref, nw_ref, w13_ref, w2_ref, *rest, mod_base, final_norm):
    o_ref = rest[-1]
    x = x_ref[...]
    shift = mod_ref[mod_base:mod_base + 1, :]
    scale = mod_ref[mod_base + 1:mod_base + 2, :]
    gate = mod_ref[mod_base + 2:mod_base + 3, :]
    hb = _rms_mod(x, nw_ref[...], shift, scale).astype(BF16)
    acc = jnp.zeros(x.shape, F32)
    for j in range(D_FF // FFN_BLOCK):
        lo = j * FFN_BLOCK
        a = _dot(hb, w13_ref[:, lo:lo + FFN_BLOCK])
        b = _dot(hb, w13_ref[:, D_FF + lo:D_FF + lo + FFN_BLOCK])
        s = (_silu(a) * b).astype(BF16)
        acc = acc + _dot(s, w2_ref[lo:lo + FFN_BLOCK, :])
    out = x + gate * (0.5 * acc)
    if final_norm:
        fw_ref = rest[0]
        ms = jnp.mean(out * out, axis=-1, keepdims=True)
        out = out * lax.rsqrt(ms + EPS) * fw_ref[...]
    o_ref[...] = out


def _ffn(x, mod, mod_base, norm_w, w13, w2, final_w):
    batch, seq, d = x.shape
    tm = FFN_TOKENS
    final_norm = final_w is not None
    in_specs = [
        pl.BlockSpec((None, tm, d), lambda b, i: (b, i, 0)),
        pl.BlockSpec((None, N_MOD, d), lambda b, i: (b, 0, 0)),
        _resident((1, d)),
        _resident(w13.shape),
        _resident(w2.shape),
    ]
    args = [x, mod, norm_w.reshape(1, d), w13, w2]
    if final_norm:
        in_specs.append(_resident((1, d)))
        args.append(final_w.reshape(1, d))
    return pl.pallas_call(
        functools.partial(_ffn_kernel, mod_base=mod_base, final_norm=final_norm),
        grid=(batch, seq // tm),
        in_specs=in_specs,
        out_specs=pl.BlockSpec((None, tm, d), lambda b, i: (b, i, 0)),
        out_shape=jax.ShapeDtypeStruct(x.shape, F32),
        compiler_params=pltpu.CompilerParams(
            dimension_semantics=("arbitrary", "arbitrary"),
            vmem_limit_bytes=V7X_VMEM_LIMIT_BYTES),
    )(*args)


def _rope(t, cos, sin):
    t1, t2 = t[:, :ROPE_HALF], t[:, ROPE_HALF:]
    return jnp.concatenate([t1 * cos - t2 * sin, t1 * sin + t2 * cos], axis=-1)


def _mixer_kernel(x_ref, mod_ref, nw_ref, cos_ref, sin_ref, idec_ref, qdec_ref, kdec_ref,
                  cdec_ref, w_in_ref, gnw_ref, w_r_ref, plin_ref, pscale_ref, w_p_ref,
                  w_out_ref, o_ref, state_ref, uext_ref, gated_ref, pmix_ref):
    i = pl.program_id(1)
    tm = x_ref.shape[0]

    @pl.when(i == 0)
    def _():
        state_ref[...] = jnp.zeros(state_ref.shape, F32)
        uext_ref[0:POOL_HALO, :] = jnp.zeros((POOL_HALO, POOL_W), F32)

    x = x_ref[...]
    shift = mod_ref[3:4, :]
    scale = mod_ref[4:5, :]
    gate = mod_ref[5:6, :]
    hb = _rms_mod(x, nw_ref[...], shift, scale).astype(BF16)
    cos = cos_ref[...]
    sin = sin_ref[...]

    for h in range(RET_HEADS):
        q = _dot(hb, w_in_ref[:, OFF_Q + h * RET_DK:OFF_Q + (h + 1) * RET_DK])
        k = _dot(hb, w_in_ref[:, OFF_K + h * RET_DK:OFF_K + (h + 1) * RET_DK])
        v = _dot(hb, w_in_ref[:, OFF_V + h * RET_DV:OFF_V + (h + 1) * RET_DV])
        g = _dot(hb, w_in_ref[:, OFF_G + h * RET_DV:OFF_G + (h + 1) * RET_DV])
        qr = _rope(q, cos, sin)
        kr = _rope(k, cos, sin) * (RET_DK ** -0.5)
        vb = v.astype(BF16)
        outs = []
        for c in range(tm // RET_CHUNK):
            rows = slice(c * RET_CHUNK, (c + 1) * RET_CHUNK)
            qc, kc, vc = qr[rows], kr[rows], vb[rows]
            scores = lax.dot_general(qc.astype(BF16), kc.astype(BF16),
                                     (((1,), (1,)), ((), ())),
                                     preferred_element_type=F32) * idec_ref[h]
            inner = _dot(scores.astype(BF16), vc)
            st = state_ref[h]
            cross = _dot((qc * qdec_ref[h]).astype(BF16), st.astype(BF16))
            kd = (kc * kdec_ref[h]).astype(BF16)
            upd = lax.dot_general(kd, vc, (((0,), (0,)), ((), ())),
                                  preferred_element_type=F32)
            state_ref[h] = cdec_ref[h] * st + upd
            outs.append(inner + cross)
        o = jnp.concatenate(outs, axis=0)
        mu = jnp.mean(o, axis=-1, keepdims=True)
        oc = o - mu
        var = jnp.mean(oc * oc, axis=-1, keepdims=True)
        ret = oc * lax.rsqrt(var + EPS) * gnw_ref[:, h * RET_DV:(h + 1) * RET_DV]
        gated_ref[:, h * RET_DV:(h + 1) * RET_DV] = (_silu(g) * ret).astype(BF16)
    y_r = _dot(gated_ref[...], w_r_ref[...])

    u = _dot(hb, w_in_ref[:, OFF_U:OFF_U + POOL_W])
    uext_ref[POOL_HALO:POOL_HALO + tm, :] = u
    pos = (i * tm + lax.broadcasted_iota(jnp.int32, (tm, 1), 0)).astype(F32)
    for gi, w in enumerate(POOL_WINDOWS):
        cols = slice(gi * POOL_GROUP_W, (gi + 1) * POOL_GROUP_W)
        ug = u[:, cols]
        wsum = ug
        for back in range(1, w):
            wsum = wsum + uext_ref[POOL_HALO - back:POOL_HALO - back + tm, cols]
        count = jnp.minimum(pos + 1.0, float(w))
        pooled = wsum / count - ug
        mixed = _dot(pooled.astype(BF16), plin_ref[gi])
        pmix_ref[:, cols] = (mixed * pscale_ref[:, cols]).astype(BF16)
    uext_ref[0:POOL_HALO, :] = uext_ref[tm:tm + POOL_HALO, :]
    y_p = _dot(pmix_ref[...], w_p_ref[...])

    a_r = _dot(hb, w_in_ref[:, OFF_AR:OFF_AR + D_MODEL])
    a_p = _dot(hb, w_in_ref[:, OFF_AP:OFF_AP + D_MODEL])
    merged = jax.nn.sigmoid(a_r) * y_r + jax.nn.sigmoid(a_p) * y_p
    o_ref[...] = x + gate * _dot(merged.astype(BF16), w_out_ref[...])


def _retention_tables():
    c = RET_CHUNK
    log_gamma = jnp.log1p(-(2.0 ** (-5.0 - jnp.arange(RET_HEADS, dtype=F32))))
    idx = jnp.arange(c, dtype=F32)
    diff = idx[:, None] - idx[None, :]
    inner = jnp.where(diff >= 0, jnp.exp(log_gamma[:, None, None] * jnp.maximum(diff, 0.0)), 0.0)
    q_decay = jnp.exp(log_gamma[:, None] * (idx + 1.0))
    k_decay = jnp.exp(log_gamma[:, None] * (c - 1.0 - idx))
    chunk_decay = jnp.exp(log_gamma * c)
    wide = (RET_HEADS, c, RET_DK)
    return (inner, jnp.broadcast_to(q_decay[:, :, None], wide),
            jnp.broadcast_to(k_decay[:, :, None], wide), chunk_decay)


def _rope_tables(seq):
    inv = 1.0 / (ROPE_BASE ** (jnp.arange(ROPE_HALF, dtype=F32) / ROPE_HALF))
    ang = jnp.arange(seq, dtype=F32)[:, None] * inv[None, :]
    return jnp.cos(ang), jnp.sin(ang)


def _mixer(x, mod, norm_w, w_in, gn_w, w_r, pool_lin, pool_scale, w_p, w_out):
    batch, seq, d = x.shape
    tm = MIX_TOKENS
    cos, sin = _rope_tables(seq)
    idec, qdec, kdec, cdec = _retention_tables()
    in_specs = [
        pl.BlockSpec((None, tm, d), lambda b, i: (b, i, 0)),
        pl.BlockSpec((None, N_MOD, d), lambda b, i: (b, 0, 0)),
        _resident((1, d)),
        pl.BlockSpec((tm, ROPE_HALF), lambda b, i: (i, 0)),
        pl.BlockSpec((tm, ROPE_HALF), lambda b, i: (i, 0)),
        _resident(idec.shape),
        _resident(qdec.shape),
        _resident(kdec.shape),
        pl.BlockSpec(memory_space=pltpu.SMEM),
        _resident(w_in.shape),
        _resident((1, RET_HEADS * RET_DV)),
        _resident(w_r.shape),
        _resident(pool_lin.shape),
        _resident((1, POOL_W)),
        _resident(w_p.shape),
        _resident(w_out.shape),
    ]
    return pl.pallas_call(
        _mixer_kernel,
        grid=(batch, seq // tm),
        in_specs=in_specs,
        out_specs=pl.BlockSpec((None, tm, d), lambda b, i: (b, i, 0)),
        out_shape=jax.ShapeDtypeStruct(x.shape, F32),
        scratch_shapes=[
            pltpu.VMEM((RET_HEADS, RET_DK, RET_DV), F32),
            pltpu.VMEM((POOL_HALO + tm, POOL_W), F32),
            pltpu.VMEM((tm, RET_HEADS * RET_DV), BF16),
            pltpu.VMEM((tm, POOL_W), BF16),
        ],
        compiler_params=pltpu.CompilerParams(
            dimension_semantics=("arbitrary", "arbitrary"),
            vmem_limit_bytes=V7X_VMEM_LIMIT_BYTES),
    )(x, mod, norm_w.reshape(1, d), cos, sin, idec, qdec, kdec, cdec, w_in,
      gn_w.reshape(1, -1), w_r, pool_lin, pool_scale.reshape(1, -1), w_p, w_out)


def kernel(x, c, ada_w, ada_b, norm_ffn1, ffn1_w13, ffn1_w2, norm_mix, w_in, ret_gn_w,
           w_ret_branch, pool_lin, pool_scale, w_pool_branch, w_out, norm_ffn2, ffn2_w13,
           ffn2_w2, norm_final):
    batch, _, d = x.shape
    depth = ada_w.shape[0]
    for l in range(depth):
        mod = _modulation(c, ada_w[l], ada_b[l]).reshape(batch, N_MOD, d)
        x = _ffn(x, mod, 0, norm_ffn1[l], ffn1_w13[l].astype(BF16), ffn1_w2[l].astype(BF16), None)
        x = _mixer(x, mod, norm_mix[l], w_in[l].astype(BF16), ret_gn_w[l],
                   w_ret_branch[l].astype(BF16), pool_lin[l].astype(BF16), pool_scale[l],
                   w_pool_branch[l].astype(BF16), w_out[l].astype(BF16))
        x = _ffn(x, mod, 6, norm_ffn2[l], ffn2_w13[l].astype(BF16), ffn2_w2[l].astype(BF16),
                 norm_final if l == depth - 1 else None)
    return x
```

```python
import functools

import jax
import jax.numpy as jnp
from jax import lax
from jax.experimental import pallas as pl
from jax.experimental.pallas import tpu as pltpu

F32 = jnp.float32
BF16 = jnp.bfloat16

D_MODEL = 1024
N_MOD = 9
EPS = 1e-6
D_FF = 2816

RET_HEADS = 4
RET_DK = 256
RET_DV = 512
RET_CHUNK = 256
ROPE_BASE = 10000.0
ROPE_HALF = RET_DK // 2

POOL_WINDOWS = (2, 4, 8, 16)
POOL_GROUP_W = 256
POOL_W = len(POOL_WINDOWS) * POOL_GROUP_W
POOL_HALO = 16

OFF_Q = 0
OFF_K = OFF_Q + RET_HEADS * RET_DK
OFF_V = OFF_K + RET_HEADS * RET_DK
OFF_G = OFF_V + RET_HEADS * RET_DV
OFF_U = OFF_G + RET_HEADS * RET_DV
OFF_AR = OFF_U + POOL_W
OFF_AP = OFF_AR + D_MODEL
IN_W = OFF_AP + D_MODEL

V7X_VMEM_LIMIT_BYTES = 58 * 1024 * 1024
V7X_MXU_WIDTH = 256

FFN_TOKENS = 1024
HEAD_ROWS = 128
FFN_BLOCK = V7X_MXU_WIDTH
MIX_TOKENS = RET_CHUNK
MOD_BLOCK = 1024


def _dot(a, b):
    return jnp.dot(a, b, preferred_element_type=F32)


def _silu(a):
    return a * jax.nn.sigmoid(a)


def _rms_mod(x, norm_w, shift, scale):
    ms = jnp.mean(x * x, axis=-1, keepdims=True)
    y = x * lax.rsqrt(ms + EPS) * norm_w
    return y * (1.0 + scale) + shift


def _resident(shape):
    zeros = (0,) * len(shape)
    return pl.BlockSpec(shape, lambda *_: zeros, pipeline_mode=pl.Buffered(1))


def _mod_kernel(c_ref, w_ref, b_ref, o_ref):
    c = c_ref[...]
    o_ref[...] = _dot(_silu(c).astype(BF16), w_ref[...].astype(BF16)) + b_ref[...]


def _modulation(c, w, b):
    batch, d = c.shape
    n = w.shape[1]
    return pl.pallas_call(
        _mod_kernel,
        grid=(n // MOD_BLOCK,),
        in_specs=[
            pl.BlockSpec((batch, d), lambda j: (0, 0)),
            pl.BlockSpec((d, MOD_BLOCK), lambda j: (0, j)),
            pl.BlockSpec((1, MOD_BLOCK), lambda j: (0, j)),
        ],
        out_specs=pl.BlockSpec((batch, MOD_BLOCK), lambda j: (0, j)),
        out_shape=jax.ShapeDtypeStruct((batch, n), F32),
        compiler_params=pltpu.CompilerParams(dimension_semantics=("arbitrary",)),
    )(c, w, b.reshape(1, n))


def _ffn_kernel(x_ref, mod_ref, nw_ref, w13_ref, w2_ref, *rest, mod_base, final_norm):
    o_ref = rest[-1]
    x = x_ref[...]
    shift = mod_ref[mod_base:mod_base + 1, :]
    scale = mod_ref[mod_base + 1:mod_base + 2, :]
    gate = mod_ref[mod_base + 2:mod_base + 3, :]
    hb = _rms_mod(x, nw_ref[...], shift, scale).astype(BF16)

    def up(lhs, lo):
        return (_dot(lhs, w13_ref[:, lo:lo + FFN_BLOCK]),
                _dot(lhs, w13_ref[:, D_FF + lo:D_FF + lo + FFN_BLOCK]))

    acc = jnp.zeros(x.shape, F32)
    for j in range(D_FF // FFN_BLOCK):
        lo = j * FFN_BLOCK
        if j == 0:
            parts = [up(hb[r:r + HEAD_ROWS], lo) for r in range(0, x.shape[0], HEAD_ROWS)]
            a = jnp.concatenate([p[0] for p in parts], axis=0)
            b = jnp.concatenate([p[1] for p in parts], axis=0)
        else:
            a, b = up(hb, lo)
        s = (_silu(a) * b).astype(BF16)
        acc = acc + _dot(s, w2_ref[lo:lo + FFN_BLOCK, :])
    out = x + gate * (0.5 * acc)
    if final_norm:
        ms = jnp.mean(out * out, axis=-1, keepdims=True)
        out = out * lax.rsqrt(ms + EPS) * rest[0][...]
    o_ref[...] = out


def _ffn(x, mod, mod_base, norm_w, w13, w2, final_w):
    batch, seq, d = x.shape
    tm = FFN_TOKENS
    final_norm = final_w is not None
    in_specs = [
        pl.BlockSpec((None, tm, d), lambda b, i: (b, i, 0)),
        pl.BlockSpec((None, N_MOD, d), lambda b, i: (b, 0, 0)),
        _resident((1, d)),
        _resident(w13.shape),
        _resident(w2.shape),
    ]
    args = [x, mod, norm_w.reshape(1, d), w13, w2]
    if final_norm:
        in_specs.append(_resident((1, d)))
        args.append(final_w.reshape(1, d))
    return pl.pallas_call(
        functools.partial(_ffn_kernel, mod_base=mod_base, final_norm=final_norm),
        grid=(batch, seq // tm),
        in_specs=in_specs,
        out_specs=pl.BlockSpec((None, tm, d), lambda b, i: (b, i, 0)),
        out_shape=jax.ShapeDtypeStruct(x.shape, F32),
        compiler_params=pltpu.CompilerParams(
            dimension_semantics=("arbitrary", "arbitrary"),
            vmem_limit_bytes=V7X_VMEM_LIMIT_BYTES),
    )(*args)


def _rope(t, cos, sin):
    t1, t2 = t[:, :ROPE_HALF], t[:, ROPE_HALF:]
    return jnp.concatenate([t1 * cos - t2 * sin, t1 * sin + t2 * cos], axis=-1)


def _mixer_kernel(x_ref, mod_ref, nw_ref, cos_ref, sin_ref, idec_ref, qdec_ref, kdec_ref,
                  cdec_ref, w_in_ref, gnw_ref, w_r_ref, plin_ref, pscale_ref, w_p_ref,
                  w_out_ref, o_ref, state_ref, uext_ref, gated_ref, pmix_ref):
    i = pl.program_id(1)
    tm = x_ref.shape[0]

    @pl.when(i == 0)
    def _():
        state_ref[...] = jnp.zeros(state_ref.shape, F32)
        uext_ref[0:POOL_HALO, :] = jnp.zeros((POOL_HALO, POOL_W), F32)

    x = x_ref[...]
    shift = mod_ref[3:4, :]
    scale = mod_ref[4:5, :]
    gate = mod_ref[5:6, :]
    hb = _rms_mod(x, nw_ref[...], shift, scale).astype(BF16)
    cos = cos_ref[...]
    sin = sin_ref[...]

    def proj(off, width):
        return _dot(hb, w_in_ref[:, off:off + width])

    qk, roped, vg, scores, cross = {}, {}, {}, {}, {}

    def proj_qk(h):
        qk[h] = (proj(OFF_Q + h * RET_DK, RET_DK), proj(OFF_K + h * RET_DK, RET_DK))

    def rope_qk(h):
        q, k = qk.pop(h)
        qr = _rope(q, cos, sin)
        kr = _rope(k, cos, sin) * (RET_DK ** -0.5)
        roped[h] = (qr.astype(BF16), (qr * qdec_ref[h]).astype(BF16),
                    kr.astype(BF16), (kr * kdec_ref[h]).astype(BF16))

    def proj_vg(h):
        v = proj(OFF_V + h * RET_DV, RET_DV)
        vg[h] = (v.astype(BF16), proj(OFF_G + h * RET_DV, RET_DV))

    def retention_matmuls(h):
        qb, qd, kb, kd = roped.pop(h)
        vb = vg[h][0]
        scores[h] = lax.dot_general(qb, kb, (((1,), (1,)), ((), ())),
                                    preferred_element_type=F32)
        st = state_ref[h]
        cross[h] = _dot(qd, st.astype(BF16))
        upd = lax.dot_general(kd, vb, (((0,), (0,)), ((), ())), preferred_element_type=F32)
        state_ref[h] = cdec_ref[h] * st + upd

    def finish_head(h):
        vb, g = vg.pop(h)
        sd = (scores.pop(h) * idec_ref[h]).astype(BF16)
        o = _dot(sd, vb) + cross.pop(h)
        mu = jnp.mean(o, axis=-1, keepdims=True)
        oc = o - mu
        var = jnp.mean(oc * oc, axis=-1, keepdims=True)
        ret = oc * lax.rsqrt(var + EPS) * gnw_ref[:, h * RET_DV:(h + 1) * RET_DV]
        gated_ref[:, h * RET_DV:(h + 1) * RET_DV] = (_silu(g) * ret).astype(BF16)

    def ret_branch(h):
        return _dot(gated_ref[:, h * RET_DV:(h + 1) * RET_DV],
                    w_r_ref[h * RET_DV:(h + 1) * RET_DV, :])

    def pool_windows(u):
        uext_ref[POOL_HALO:POOL_HALO + tm, :] = u
        pos = (i * tm + lax.broadcasted_iota(jnp.int32, (tm, 1), 0)).astype(F32)
        pooled = []
        for gi, w in enumerate(POOL_WINDOWS):
            cols = slice(gi * POOL_GROUP_W, (gi + 1) * POOL_GROUP_W)
            s = uext_ref[:, cols]
            span = 1
            while span < w:
                s = s + pltpu.roll(s, span, axis=0)
                span *= 2
            count = jnp.minimum(pos + 1.0, float(w))
            pooled.append((s[POOL_HALO:, :] / count - u[:, cols]).astype(BF16))
        uext_ref[0:POOL_HALO, :] = uext_ref[tm:tm + POOL_HALO, :]
        return pooled

    def pool_linear(pooled):
        for gi in range(len(POOL_WINDOWS)):
            cols = slice(gi * POOL_GROUP_W, (gi + 1) * POOL_GROUP_W)
            mixed = _dot(pooled[gi], plin_ref[gi])
            pmix_ref[:, cols] = (mixed * pscale_ref[:, cols]).astype(BF16)

    proj_qk(0)
    u = proj(OFF_U, POOL_W)
    proj_vg(0)
    rope_qk(0)
    pooled = pool_windows(u)
    y_r = None
    for h in range(RET_HEADS):
        last = h + 1 == RET_HEADS
        if not last:
            proj_qk(h + 1)
        retention_matmuls(h)
        if not last:
            rope_qk(h + 1)
            proj_vg(h + 1)
        else:
            a_r = proj(OFF_AR, D_MODEL)
            a_p = proj(OFF_AP, D_MODEL)
        finish_head(h)
        if h == 0:
            pool_linear(pooled)
        if h >= 1:
            part = ret_branch(h - 1)
            y_r = part if y_r is None else y_r + part
    y_p = _dot(pmix_ref[...], w_p_ref[...])
    y_r = y_r + ret_branch(RET_HEADS - 1)

    half = tm // 2
    for r in range(2):
        rows = slice(r * half, (r + 1) * half)
        merged = (jax.nn.sigmoid(a_r[rows]) * y_r[rows]
                  + jax.nn.sigmoid(a_p[rows]) * y_p[rows])
        o_ref[rows, :] = x[rows] + gate * _dot(merged.astype(BF16), w_out_ref[...])


def _retention_tables():
    c = RET_CHUNK
    log_gamma = jnp.log1p(-(2.0 ** (-5.0 - jnp.arange(RET_HEADS, dtype=F32))))
    idx = jnp.arange(c, dtype=F32)
    diff = idx[:, None] - idx[None, :]
    inner = jnp.where(diff >= 0, jnp.exp(log_gamma[:, None, None] * jnp.maximum(diff, 0.0)), 0.0)
    q_decay = jnp.exp(log_gamma[:, None] * (idx + 1.0))
    k_decay = jnp.exp(log_gamma[:, None] * (c - 1.0 - idx))
    chunk_decay = jnp.exp(log_gamma * c)
    wide = (RET_HEADS, c, RET_DK)
    return (inner, jnp.broadcast_to(q_decay[:, :, None], wide),
            jnp.broadcast_to(k_decay[:, :, None], wide), chunk_decay)


def _rope_tables(seq):
    inv = 1.0 / (ROPE_BASE ** (jnp.arange(ROPE_HALF, dtype=F32) / ROPE_HALF))
    ang = jnp.arange(seq, dtype=F32)[:, None] * inv[None, :]
    return jnp.cos(ang), jnp.sin(ang)


def _mixer(x, mod, norm_w, w_in, gn_w, w_r, pool_lin, pool_scale, w_p, w_out):
    batch, seq, d = x.shape
    tm = MIX_TOKENS
    cos, sin = _rope_tables(seq)
    idec, qdec, kdec, cdec = _retention_tables()
    in_specs = [
        pl.BlockSpec((None, tm, d), lambda b, i: (b, i, 0)),
        pl.BlockSpec((None, N_MOD, d), lambda b, i: (b, 0, 0)),
        _resident((1, d)),
        pl.BlockSpec((tm, ROPE_HALF), lambda b, i: (i, 0)),
        pl.BlockSpec((tm, ROPE_HALF), lambda b, i: (i, 0)),
        _resident(idec.shape),
        _resident(qdec.shape),
        _resident(kdec.shape),
        pl.BlockSpec(memory_space=pltpu.SMEM),
        _resident(w_in.shape),
        _resident((1, RET_HEADS * RET_DV)),
        _resident(w_r.shape),
        _resident(pool_lin.shape),
        _resident((1, POOL_W)),
        _resident(w_p.shape),
        _resident(w_out.shape),
    ]
    return pl.pallas_call(
        _mixer_kernel,
        grid=(batch, seq // tm),
        in_specs=in_specs,
        out_specs=pl.BlockSpec((None, tm, d), lambda b, i: (b, i, 0)),
        out_shape=jax.ShapeDtypeStruct(x.shape, F32),
        scratch_shapes=[
            pltpu.VMEM((RET_HEADS, RET_DK, RET_DV), F32),
            pltpu.VMEM((POOL_HALO + tm, POOL_W), F32),
            pltpu.VMEM((tm, RET_HEADS * RET_DV), BF16),
            pltpu.VMEM((tm, POOL_W), BF16),
        ],
        compiler_params=pltpu.CompilerParams(
            dimension_semantics=("arbitrary", "arbitrary"),
            vmem_limit_bytes=V7X_VMEM_LIMIT_BYTES),
    )(x, mod, norm_w.reshape(1, d), cos, sin, idec, qdec, kdec, cdec, w_in,
      gn_w.reshape(1, -1), w_r, pool_lin, pool_scale.reshape(1, -1), w_p, w_out)


def kernel(x, c, ada_w, ada_b, norm_ffn1, ffn1_w13, ffn1_w2, norm_mix, w_in, ret_gn_w,
           w_ret_branch, pool_lin, pool_scale, w_pool_branch, w_out, norm_ffn2, ffn2_w13,
           ffn2_w2, norm_final):
    batch, _, d = x.shape
    depth = ada_w.shape[0]
    for l in range(depth):
        mod = _modulation(c, ada_w[l], ada_b[l]).reshape(batch, N_MOD, d)
        x = _ffn(x, mod, 0, norm_ffn1[l], ffn1_w13[l].astype(BF16), ffn1_w2[l].astype(BF16), None)
        x = _mixer(x, mod, norm_mix[l], w_in[l].astype(BF16), ret_gn_w[l],
                   w_ret_branch[l].astype(BF16), pool_lin[l].astype(BF16), pool_scale[l],
                   w_pool_branch[l].astype(BF16), w_out[l].astype(BF16))
        x = _ffn(x, mod, 6, norm_ffn2[l], ffn2_w13[l].astype(BF16), ffn2_w2[l].astype(BF16),
                 norm_final if l == depth - 1 else None)
    return x
```

```python
import functools

import jax
import jax.numpy as jnp
from jax import lax
from jax.experimental import pallas as pl
from jax.experimental.pallas import tpu as pltpu

F32 = jnp.float32
BF16 = jnp.bfloat16

D_MODEL = 1024
N_MOD = 9
EPS = 1e-6
D_FF = 2816

RET_HEADS = 4
RET_DK = 256
RET_DV = 512
RET_CHUNK = 256
ROPE_BASE = 10000.0
ROPE_HALF = RET_DK // 2

POOL_WINDOWS = (2, 4, 8, 16)
POOL_GROUP_W = 256
POOL_W = len(POOL_WINDOWS) * POOL_GROUP_W
POOL_HALO = 16

OFF_Q = 0
OFF_K = OFF_Q + RET_HEADS * RET_DK
OFF_V = OFF_K + RET_HEADS * RET_DK
OFF_G = OFF_V + RET_HEADS * RET_DV
OFF_U = OFF_G + RET_HEADS * RET_DV
OFF_AR = OFF_U + POOL_W
OFF_AP = OFF_AR + D_MODEL
IN_W = OFF_AP + D_MODEL

V7X_VMEM_LIMIT_BYTES = 58 * 1024 * 1024
V7X_MXU_WIDTH = 256

FFN_TOKENS = 1024
HEAD_ROWS = 128
FFN_BLOCK = V7X_MXU_WIDTH
MIX_TOKENS = RET_CHUNK
MOD_BLOCK = 1024
STAGE_ROWS = 256
STAGE_ROWS_WIDE = 64


def _dot(a, b):
    return jnp.dot(a, b, preferred_element_type=F32)


def _silu(a):
    return a * jax.nn.sigmoid(a)


def _rms_mod(x, norm_w, shift, scale):
    ms = jnp.mean(x * x, axis=-1, keepdims=True)
    y = x * lax.rsqrt(ms + EPS) * norm_w
    return y * (1.0 + scale) + shift


def _resident(shape):
    zeros = (0,) * len(shape)
    return pl.BlockSpec(shape, lambda *_: zeros, pipeline_mode=pl.Buffered(1))


def _is_first_step():
    return jnp.logical_and(pl.program_id(0) == 0, pl.program_id(1) == 0)


def _load_weight_bf16(src_ref, dst_ref, stage_ref, sem_ref):
    rows = stage_ref.shape[1]
    n = src_ref.shape[0] // rows

    def copy(c):
        return pltpu.make_async_copy(src_ref.at[pl.ds(c * rows, rows)], stage_ref.at[c % 2],
                                     sem_ref.at[c % 2])

    for c in range(min(2, n)):
        copy(c).start()
    for c in range(n):
        copy(c).wait()
        dst_ref[c * rows:(c + 1) * rows, :] = stage_ref[c % 2].astype(BF16)
        if c + 2 < n:
            copy(c + 2).start()


def _mod_kernel(c_ref, w_ref, b_ref, o_ref):
    c = c_ref[...]
    o_ref[...] = _dot(_silu(c).astype(BF16), w_ref[...].astype(BF16)) + b_ref[...]


def _modulation(c, w, b):
    batch, d = c.shape
    n = w.shape[1]
    return pl.pallas_call(
        _mod_kernel,
        grid=(n // MOD_BLOCK,),
        in_specs=[
            pl.BlockSpec((batch, d), lambda j: (0, 0)),
            pl.BlockSpec((d, MOD_BLOCK), lambda j: (0, j)),
            pl.BlockSpec((1, MOD_BLOCK), lambda j: (0, j)),
        ],
        out_specs=pl.BlockSpec((batch, MOD_BLOCK), lambda j: (0, j)),
        out_shape=jax.ShapeDtypeStruct((batch, n), F32),
        compiler_params=pltpu.CompilerParams(dimension_semantics=("arbitrary",)),
    )(c, w, b.reshape(1, n))


def _ffn_kernel(x_ref, mod_ref, nw_ref, w13_hbm, w2_hbm, *rest, mod_base, final_norm):
    o_ref, w13_ref, w2_ref, stage13_ref, stage2_ref, sem_ref = rest[-6:]

    @pl.when(_is_first_step())
    def _():
        _load_weight_bf16(w13_hbm, w13_ref, stage13_ref, sem_ref)
        _load_weight_bf16(w2_hbm, w2_ref, stage2_ref, sem_ref)

    x = x_ref[...]
    shift = mod_ref[mod_base:mod_base + 1, :]
    scale = mod_ref[mod_base + 1:mod_base + 2, :]
    gate = mod_ref[mod_base + 2:mod_base + 3, :]
    hb = _rms_mod(x, nw_ref[...], shift, scale).astype(BF16)

    def up(lhs, lo):
        return (_dot(lhs, w13_ref[:, lo:lo + FFN_BLOCK]),
                _dot(lhs, w13_ref[:, D_FF + lo:D_FF + lo + FFN_BLOCK]))

    acc = jnp.zeros(x.shape, F32)
    for j in range(D_FF // FFN_BLOCK):
        lo = j * FFN_BLOCK
        if j == 0:
            parts = [up(hb[r:r + HEAD_ROWS], lo) for r in range(0, x.shape[0], HEAD_ROWS)]
            a = jnp.concatenate([p[0] for p in parts], axis=0)
            b = jnp.concatenate([p[1] for p in parts], axis=0)
        else:
            a, b = up(hb, lo)
        s = (_silu(a) * b).astype(BF16)
        acc = acc + _dot(s, w2_ref[lo:lo + FFN_BLOCK, :])
    out = x + gate * (0.5 * acc)
    if final_norm:
        ms = jnp.mean(out * out, axis=-1, keepdims=True)
        out = out * lax.rsqrt(ms + EPS) * rest[0][...]
    o_ref[...] = out


def _ffn(x, mod, mod_base, norm_w, w13, w2, final_w):
    batch, seq, d = x.shape
    tm = FFN_TOKENS
    final_norm = final_w is not None
    in_specs = [
        pl.BlockSpec((None, tm, d), lambda b, i: (b, i, 0)),
        pl.BlockSpec((None, N_MOD, d), lambda b, i: (b, 0, 0)),
        _resident((1, d)),
        pl.BlockSpec(memory_space=pl.ANY),
        pl.BlockSpec(memory_space=pl.ANY),
    ]
    args = [x, mod, norm_w.reshape(1, d), w13, w2]
    if final_norm:
        in_specs.append(_resident((1, d)))
        args.append(final_w.reshape(1, d))
    return pl.pallas_call(
        functools.partial(_ffn_kernel, mod_base=mod_base, final_norm=final_norm),
        grid=(batch, seq // tm),
        in_specs=in_specs,
        out_specs=pl.BlockSpec((None, tm, d), lambda b, i: (b, i, 0)),
        out_shape=jax.ShapeDtypeStruct(x.shape, F32),
        scratch_shapes=[
            pltpu.VMEM(w13.shape, BF16),
            pltpu.VMEM(w2.shape, BF16),
            pltpu.VMEM((2, STAGE_ROWS_WIDE, w13.shape[1]), F32),
            pltpu.VMEM((2, STAGE_ROWS, w2.shape[1]), F32),
            pltpu.SemaphoreType.DMA((2,)),
        ],
        compiler_params=pltpu.CompilerParams(
            dimension_semantics=("arbitrary", "arbitrary"),
            vmem_limit_bytes=V7X_VMEM_LIMIT_BYTES),
    )(*args)


def _rope(t, cos, sin):
    t1, t2 = t[:, :ROPE_HALF], t[:, ROPE_HALF:]
    return jnp.concatenate([t1 * cos - t2 * sin, t1 * sin + t2 * cos], axis=-1)


def _mixer_kernel(x_ref, mod_ref, nw_ref, cos_ref, sin_ref, idec_ref, qdec_ref, kdec_ref,
                  cdec_ref, w_in_hbm, gnw_ref, w_r_hbm, plin_hbm, pscale_ref, w_p_hbm,
                  w_out_hbm, o_ref, state_ref, uext_ref, gated_ref, pmix_ref,
                  w_in_ref, w_r_ref, plin_ref, w_p_ref, w_out_ref,
                  stage_in_ref, stage_ref, stage_plin_ref, sem_ref):
    i = pl.program_id(1)
    tm = x_ref.shape[0]

    @pl.when(_is_first_step())
    def _():
        _load_weight_bf16(w_in_hbm, w_in_ref, stage_in_ref, sem_ref)
        _load_weight_bf16(w_r_hbm, w_r_ref, stage_ref, sem_ref)
        _load_weight_bf16(w_p_hbm, w_p_ref, stage_ref, sem_ref)
        _load_weight_bf16(w_out_hbm, w_out_ref, stage_ref, sem_ref)
        _load_weight_bf16(plin_hbm, plin_ref, stage_plin_ref, sem_ref)

    @pl.when(i == 0)
    def _():
        state_ref[...] = jnp.zeros(state_ref.shape, F32)
        uext_ref[0:POOL_HALO, :] = jnp.zeros((POOL_HALO, POOL_W), F32)

    x = x_ref[...]
    shift = mod_ref[3:4, :]
    scale = mod_ref[4:5, :]
    gate = mod_ref[5:6, :]
    hb = _rms_mod(x, nw_ref[...], shift, scale).astype(BF16)
    cos = cos_ref[...]
    sin = sin_ref[...]

    def proj(off, width):
        return _dot(hb, w_in_ref[:, off:off + width])

    qk, roped, vg, scores, cross = {}, {}, {}, {}, {}

    def proj_qk(h):
        qk[h] = (proj(OFF_Q + h * RET_DK, RET_DK), proj(OFF_K + h * RET_DK, RET_DK))

    def rope_qk(h):
        q, k = qk.pop(h)
        qr = _rope(q, cos, sin)
        kr = _rope(k, cos, sin) * (RET_DK ** -0.5)
        roped[h] = (qr.astype(BF16), (qr * qdec_ref[h]).astype(BF16),
                    kr.astype(BF16), (kr * kdec_ref[h]).astype(BF16))

    def proj_vg(h):
        v = proj(OFF_V + h * RET_DV, RET_DV)
        vg[h] = (v.astype(BF16), proj(OFF_G + h * RET_DV, RET_DV))

    def retention_matmuls(h):
        qb, qd, kb, kd = roped.pop(h)
        vb = vg[h][0]
        scores[h] = lax.dot_general(qb, kb, (((1,), (1,)), ((), ())),
                                    preferred_element_type=F32)
        st = state_ref[h]
        cross[h] = _dot(qd, st.astype(BF16))
        upd = lax.dot_general(kd, vb, (((0,), (0,)), ((), ())), preferred_element_type=F32)
        state_ref[h] = cdec_ref[h] * st + upd

    def finish_head(h):
        vb, g = vg.pop(h)
        sd = (scores.pop(h) * idec_ref[h]).astype(BF16)
        o = _dot(sd, vb) + cross.pop(h)
        mu = jnp.mean(o, axis=-1, keepdims=True)
        oc = o - mu
        var = jnp.mean(oc * oc, axis=-1, keepdims=True)
        ret = oc * lax.rsqrt(var + EPS) * gnw_ref[:, h * RET_DV:(h + 1) * RET_DV]
        gated_ref[:, h * RET_DV:(h + 1) * RET_DV] = (_silu(g) * ret).astype(BF16)

    def ret_branch(h):
        return _dot(gated_ref[:, h * RET_DV:(h + 1) * RET_DV],
                    w_r_ref[h * RET_DV:(h + 1) * RET_DV, :])

    def pool_windows(u):
        uext_ref[POOL_HALO:POOL_HALO + tm, :] = u
        pos = (i * tm + lax.broadcasted_iota(jnp.int32, (tm, 1), 0)).astype(F32)
        pooled = []
        for gi, w in enumerate(POOL_WINDOWS):
            cols = slice(gi * POOL_GROUP_W, (gi + 1) * POOL_GROUP_W)
            s = uext_ref[:, cols]
            span = 1
            while span < w:
                s = s + pltpu.roll(s, span, axis=0)
                span *= 2
            count = jnp.minimum(pos + 1.0, float(w))
            pooled.append((s[POOL_HALO:, :] / count - u[:, cols]).astype(BF16))
        uext_ref[0:POOL_HALO, :] = uext_ref[tm:tm + POOL_HALO, :]
        return pooled

    def pool_linear(pooled):
        for gi in range(len(POOL_WINDOWS)):
            cols = slice(gi * POOL_GROUP_W, (gi + 1) * POOL_GROUP_W)
            mixed = _dot(pooled[gi], plin_ref[cols, :])
            pmix_ref[:, cols] = (mixed * pscale_ref[:, cols]).astype(BF16)

    proj_qk(0)
    u = proj(OFF_U, POOL_W)
    proj_vg(0)
    rope_qk(0)
    pooled = pool_windows(u)
    y_r = None
    for h in range(RET_HEADS):
        last = h + 1 == RET_HEADS
        if not last:
            proj_qk(h + 1)
        retention_matmuls(h)
        if not last:
            rope_qk(h + 1)
            proj_vg(h + 1)
        else:
            a_r = proj(OFF_AR, D_MODEL)
            a_p = proj(OFF_AP, D_MODEL)
        finish_head(h)
        if h == 0:
            pool_linear(pooled)
        if h >= 1:
            part = ret_branch(h - 1)
            y_r = part if y_r is None else y_r + part
    y_p = _dot(pmix_ref[...], w_p_ref[...])
    y_r = y_r + ret_branch(RET_HEADS - 1)

    half = tm // 2
    for r in range(2):
        rows = slice(r * half, (r + 1) * half)
        merged = (jax.nn.sigmoid(a_r[rows]) * y_r[rows]
                  + jax.nn.sigmoid(a_p[rows]) * y_p[rows])
        o_ref[rows, :] = x[rows] + gate * _dot(merged.astype(BF16), w_out_ref[...])


def _retention_tables():
    c = RET_CHUNK
    log_gamma = jnp.log1p(-(2.0 ** (-5.0 - jnp.arange(RET_HEADS, dtype=F32))))
    idx = jnp.arange(c, dtype=F32)
    diff = idx[:, None] - idx[None, :]
    inner = jnp.where(diff >= 0, jnp.exp(log_gamma[:, None, None] * jnp.maximum(diff, 0.0)), 0.0)
    q_decay = jnp.exp(log_gamma[:, None] * (idx + 1.0))
    k_decay = jnp.exp(log_gamma[:, None] * (c - 1.0 - idx))
    chunk_decay = jnp.exp(log_gamma * c)
    wide = (RET_HEADS, c, RET_DK)
    return (inner, jnp.broadcast_to(q_decay[:, :, None], wide),
            jnp.broadcast_to(k_decay[:, :, None], wide), chunk_decay)


def _rope_tables(seq):
    inv = 1.0 / (ROPE_BASE ** (jnp.arange(ROPE_HALF, dtype=F32) / ROPE_HALF))
    ang = jnp.arange(seq, dtype=F32)[:, None] * inv[None, :]
    return jnp.cos(ang), jnp.sin(ang)


def _mixer(x, mod, norm_w, w_in, gn_w, w_r, pool_lin, pool_scale, w_p, w_out):
    batch, seq, d = x.shape
    tm = MIX_TOKENS
    cos, sin = _rope_tables(seq)
    idec, qdec, kdec, cdec = _retention_tables()
    in_specs = [
        pl.BlockSpec((None, tm, d), lambda b, i: (b, i, 0)),
        pl.BlockSpec((None, N_MOD, d), lambda b, i: (b, 0, 0)),
        _resident((1, d)),
        pl.BlockSpec((tm, ROPE_HALF), lambda b, i: (i, 0)),
        pl.BlockSpec((tm, ROPE_HALF), lambda b, i: (i, 0)),
        _resident(idec.shape),
        _resident(qdec.shape),
        _resident(kdec.shape),
        pl.BlockSpec(memory_space=pltpu.SMEM),
        pl.BlockSpec(memory_space=pl.ANY),
        _resident((1, RET_HEADS * RET_DV)),
        pl.BlockSpec(memory_space=pl.ANY),
        pl.BlockSpec(memory_space=pl.ANY),
        _resident((1, POOL_W)),
        pl.BlockSpec(memory_space=pl.ANY),
        pl.BlockSpec(memory_space=pl.ANY),
    ]
    pool_lin = pool_lin.reshape(POOL_W, POOL_GROUP_W)
    return pl.pallas_call(
        _mixer_kernel,
        grid=(batch, seq // tm),
        in_specs=in_specs,
        out_specs=pl.BlockSpec((None, tm, d), lambda b, i: (b, i, 0)),
        out_shape=jax.ShapeDtypeStruct(x.shape, F32),
        scratch_shapes=[
            pltpu.VMEM((RET_HEADS, RET_DK, RET_DV), F32),
            pltpu.VMEM((POOL_HALO + tm, POOL_W), F32),
            pltpu.VMEM((tm, RET_HEADS * RET_DV), BF16),
            pltpu.VMEM((tm, POOL_W), BF16),
            pltpu.VMEM(w_in.shape, BF16),
            pltpu.VMEM(w_r.shape, BF16),
            pltpu.VMEM(pool_lin.shape, BF16),
            pltpu.VMEM(w_p.shape, BF16),
            pltpu.VMEM(w_out.shape, BF16),
            pltpu.VMEM((2, STAGE_ROWS_WIDE, w_in.shape[1]), F32),
            pltpu.VMEM((2, STAGE_ROWS, d), F32),
            pltpu.VMEM((2, STAGE_ROWS, POOL_GROUP_W), F32),
            pltpu.SemaphoreType.DMA((2,)),
        ],
        compiler_params=pltpu.CompilerParams(
            dimension_semantics=("arbitrary", "arbitrary"),
            vmem_limit_bytes=V7X_VMEM_LIMIT_BYTES),
    )(x, mod, norm_w.reshape(1, d), cos, sin, idec, qdec, kdec, cdec, w_in,
      gn_w.reshape(1, -1), w_r, pool_lin, pool_scale.reshape(1, -1), w_p, w_out)


def kernel(x, c, ada_w, ada_b, norm_ffn1, ffn1_w13, ffn1_w2, norm_mix, w_in, ret_gn_w,
           w_ret_branch, pool_lin, pool_scale, w_pool_branch, w_out, norm_ffn2, ffn2_w13,
           ffn2_w2, norm_final):
    batch, _, d = x.shape
    depth = ada_w.shape[0]
    for l in range(depth):
        mod = _modulation(c, ada_w[l], ada_b[l]).reshape(batch, N_MOD, d)
        x = _ffn(x, mod, 0, norm_ffn1[l], ffn1_w13[l], ffn1_w2[l], None)
        x = _mixer(x, mod, norm_mix[l], w_in[l], ret_gn_w[l], w_ret_branch[l], pool_lin[l],
                   pool_scale[l], w_pool_branch[l], w_out[l])
        x = _ffn(x, mod, 6, norm_ffn2[l], ffn2_w13[l], ffn2_w2[l],
                 norm_final if l == depth - 1 else None)
    return x
```

```python
import functools

import jax
import jax.numpy as jnp
from jax import lax
from jax.experimental import pallas as pl
from jax.experimental.pallas import tpu as pltpu

F32 = jnp.float32
BF16 = jnp.bfloat16

D_MODEL = 1024
N_MOD = 9
EPS = 1e-6
D_FF = 2816

RET_HEADS = 4
RET_DK = 256
RET_DV = 512
RET_CHUNK = 256
ROPE_BASE = 10000.0
ROPE_HALF = RET_DK // 2

POOL_WINDOWS = (2, 4, 8, 16)
POOL_GROUP_W = 256
POOL_W = len(POOL_WINDOWS) * POOL_GROUP_W
POOL_HALO = 16

OFF_Q = 0
OFF_K = OFF_Q + RET_HEADS * RET_DK
OFF_V = OFF_K + RET_HEADS * RET_DK
OFF_G = OFF_V + RET_HEADS * RET_DV
OFF_U = OFF_G + RET_HEADS * RET_DV
OFF_AR = OFF_U + POOL_W
OFF_AP = OFF_AR + D_MODEL
IN_W = OFF_AP + D_MODEL

V7X_VMEM_LIMIT_BYTES = 58 * 1024 * 1024
V7X_MXU_WIDTH = 256

FFN_TOKENS = 1024
HEAD_ROWS = 128
TAIL_BLOCKS = 2
TAIL_ROWS = 256
FFN_BLOCK = V7X_MXU_WIDTH
MIX_TOKENS = 2 * RET_CHUNK
MOD_BLOCK = 1024
STAGE_SLOTS = 4
FFN_STAGE_ROWS = (64, 256)
MIX_STAGE_ROWS = (16, 128)


def _dot(a, b):
    return jnp.dot(a, b, preferred_element_type=F32)


def _silu(a):
    return a * jax.nn.sigmoid(a)


def _rms_mod(x, norm_w, shift, scale):
    ms = jnp.mean(x * x, axis=-1, keepdims=True)
    y = x * lax.rsqrt(ms + EPS) * norm_w
    return y * (1.0 + scale) + shift


def _resident(shape):
    zeros = (0,) * len(shape)
    return pl.BlockSpec(shape, lambda *_: zeros, pipeline_mode=pl.Buffered(1))


def _is_first_step():
    return jnp.logical_and(pl.program_id(0) == 0, pl.program_id(1) == 0)


def _load_weight_bf16(src_ref, dst_ref, stage_ref, sem_ref):
    slots, rows = stage_ref.shape[:2]
    n = src_ref.shape[0] // rows

    def copy(c):
        return pltpu.make_async_copy(src_ref.at[pl.ds(c * rows, rows)], stage_ref.at[c % slots],
                                     sem_ref.at[c % slots])

    for c in range(min(slots, n)):
        copy(c).start()
    for c in range(n):
        copy(c).wait()
        dst_ref[c * rows:(c + 1) * rows, :] = stage_ref[c % slots].astype(BF16)
        if c + slots < n:
            copy(c + slots).start()


def _mod_kernel(c_ref, w_ref, b_ref, o_ref):
    c = c_ref[...]
    o_ref[...] = _dot(_silu(c).astype(BF16), w_ref[...].astype(BF16)) + b_ref[...]


def _modulation(c, w, b):
    batch, d = c.shape
    n = w.shape[1]
    return pl.pallas_call(
        _mod_kernel,
        grid=(n // MOD_BLOCK,),
        in_specs=[
            pl.BlockSpec((batch, d), lambda j: (0, 0)),
            pl.BlockSpec((d, MOD_BLOCK), lambda j: (0, j)),
            pl.BlockSpec((1, MOD_BLOCK), lambda j: (0, j)),
        ],
        out_specs=pl.BlockSpec((batch, MOD_BLOCK), lambda j: (0, j)),
        out_shape=jax.ShapeDtypeStruct((batch, n), F32),
        compiler_params=pltpu.CompilerParams(dimension_semantics=("arbitrary",)),
    )(c, w, b.reshape(1, n))


def _ffn_kernel(x_ref, mod_ref, nw_ref, w13_hbm, w2_hbm, *rest, mod_base, final_norm):
    o_ref, w13_ref, w2_ref, stage13_ref, stage2_ref, sem_ref = rest[-6:]

    @pl.when(_is_first_step())
    def _():
        _load_weight_bf16(w13_hbm, w13_ref, stage13_ref, sem_ref)
        _load_weight_bf16(w2_hbm, w2_ref, stage2_ref, sem_ref)

    x = x_ref[...]
    shift = mod_ref[mod_base:mod_base + 1, :]
    scale = mod_ref[mod_base + 1:mod_base + 2, :]
    gate = mod_ref[mod_base + 2:mod_base + 3, :]
    hb = _rms_mod(x, nw_ref[...], shift, scale).astype(BF16)

    def up(lhs, lo):
        return (_dot(lhs, w13_ref[:, lo:lo + FFN_BLOCK]),
                _dot(lhs, w13_ref[:, D_FF + lo:D_FF + lo + FFN_BLOCK]))

    def down(a, b, lo):
        return _dot((_silu(a) * b).astype(BF16), w2_ref[lo:lo + FFN_BLOCK, :])

    n_blocks = D_FF // FFN_BLOCK
    tail_blocks = TAIL_BLOCKS if final_norm else 0
    acc = jnp.zeros(x.shape, F32)
    for j in range(n_blocks - tail_blocks):
        lo = j * FFN_BLOCK
        if j == 0:
            parts = [up(hb[r:r + HEAD_ROWS], lo) for r in range(0, x.shape[0], HEAD_ROWS)]
            a = jnp.concatenate([p[0] for p in parts], axis=0)
            b = jnp.concatenate([p[1] for p in parts], axis=0)
        else:
            a, b = up(hb, lo)
        acc = acc + down(a, b, lo)
    chunks = [slice(r, r + TAIL_ROWS) for r in range(0, x.shape[0], TAIL_ROWS)]
    tail = [j * FFN_BLOCK for j in range(n_blocks - tail_blocks, n_blocks)]
    ups = {(0, lo): up(hb[chunks[0]], lo) for lo in tail}
    for r, rows in enumerate(chunks):
        acc_r = acc[rows]
        for lo in tail:
            acc_r = acc_r + down(*ups.pop((r, lo)), lo)
            if r + 1 < len(chunks):
                ups[(r + 1, lo)] = up(hb[chunks[r + 1]], lo)
        out = x[rows] + gate * (0.5 * acc_r)
        if final_norm:
            ms = jnp.mean(out * out, axis=-1, keepdims=True)
            out = out * lax.rsqrt(ms + EPS) * rest[0][...]
        o_ref[rows, :] = out


def _ffn(x, mod, mod_base, norm_w, w13, w2, final_w):
    batch, seq, d = x.shape
    tm = FFN_TOKENS
    final_norm = final_w is not None
    in_specs = [
        pl.BlockSpec((None, tm, d), lambda b, i: (b, i, 0)),
        pl.BlockSpec((None, N_MOD, d), lambda b, i: (b, 0, 0)),
        _resident((1, d)),
        pl.BlockSpec(memory_space=pl.ANY),
        pl.BlockSpec(memory_space=pl.ANY),
    ]
    args = [x, mod, norm_w.reshape(1, d), w13, w2]
    if final_norm:
        in_specs.append(_resident((1, d)))
        args.append(final_w.reshape(1, d))
    return pl.pallas_call(
        functools.partial(_ffn_kernel, mod_base=mod_base, final_norm=final_norm),
        grid=(batch, seq // tm),
        in_specs=in_specs,
        out_specs=pl.BlockSpec((None, tm, d), lambda b, i: (b, i, 0)),
        out_shape=jax.ShapeDtypeStruct(x.shape, F32),
        scratch_shapes=[
            pltpu.VMEM(w13.shape, BF16),
            pltpu.VMEM(w2.shape, BF16),
            pltpu.VMEM((STAGE_SLOTS, FFN_STAGE_ROWS[0], w13.shape[1]), F32),
            pltpu.VMEM((STAGE_SLOTS, FFN_STAGE_ROWS[1], w2.shape[1]), F32),
            pltpu.SemaphoreType.DMA((STAGE_SLOTS,)),
        ],
        compiler_params=pltpu.CompilerParams(
            dimension_semantics=("arbitrary", "arbitrary"),
            vmem_limit_bytes=V7X_VMEM_LIMIT_BYTES),
    )(*args)


def _rope(t, cos, sin):
    t1, t2 = t[:, :ROPE_HALF], t[:, ROPE_HALF:]
    return jnp.concatenate([t1 * cos - t2 * sin, t1 * sin + t2 * cos], axis=-1)


def _mixer_kernel(x_ref, mod_ref, nw_ref, cos_ref, sin_ref, idec_ref, qdec_ref, kdec_ref,
                  cdec_ref, w_in_hbm, gnw_ref, w_r_hbm, plin_hbm, pscale_ref, w_p_hbm,
                  w_out_hbm, o_ref, state_ref, uext_ref, gated_ref, pmix_ref,
                  w_in_ref, w_r_ref, plin_ref, w_p_ref, w_out_ref,
                  stage_in_ref, stage_ref, stage_plin_ref, sem_ref):
    i = pl.program_id(1)
    tm = x_ref.shape[0]
    n_chunks = tm // RET_CHUNK

    @pl.when(_is_first_step())
    def _():
        _load_weight_bf16(w_in_hbm, w_in_ref, stage_in_ref, sem_ref)
        _load_weight_bf16(w_r_hbm, w_r_ref, stage_ref, sem_ref)
        _load_weight_bf16(w_p_hbm, w_p_ref, stage_ref, sem_ref)
        _load_weight_bf16(w_out_hbm, w_out_ref, stage_ref, sem_ref)
        _load_weight_bf16(plin_hbm, plin_ref, stage_plin_ref, sem_ref)

    @pl.when(i == 0)
    def _():
        state_ref[...] = jnp.zeros(state_ref.shape, F32)
        uext_ref[0:POOL_HALO, :] = jnp.zeros((POOL_HALO, POOL_W), F32)

    x = x_ref[...]
    shift = mod_ref[3:4, :]
    scale = mod_ref[4:5, :]
    gate = mod_ref[5:6, :]
    hb = _rms_mod(x, nw_ref[...], shift, scale).astype(BF16)
    cos = cos_ref[...]
    sin = sin_ref[...]

    def proj(off, width):
        return _dot(hb, w_in_ref[:, off:off + width])

    qk, roped, vg, scores, cross = {}, {}, {}, {}, {}

    def proj_qk(h):
        qk[h] = (proj(OFF_Q + h * RET_DK, RET_DK), proj(OFF_K + h * RET_DK, RET_DK))

    def chunk_rows(c):
        return slice(c * RET_CHUNK, (c + 1) * RET_CHUNK)

    def rope_qk(h):
        q, k = qk.pop(h)
        qr = _rope(q, cos, sin)
        kr = _rope(k, cos, sin) * (RET_DK ** -0.5)
        qdec, kdec = qdec_ref[h], kdec_ref[h]
        roped[h] = [(qr[chunk_rows(c)].astype(BF16), (qr[chunk_rows(c)] * qdec).astype(BF16),
                     kr[chunk_rows(c)].astype(BF16), (kr[chunk_rows(c)] * kdec).astype(BF16))
                    for c in range(n_chunks)]

    def proj_vg(h):
        v = proj(OFF_V + h * RET_DV, RET_DV)
        vg[h] = (v.astype(BF16), proj(OFF_G + h * RET_DV, RET_DV))

    def state_step(h, c):
        _, qd, _, kd = roped[h][c]
        st = state_ref[h]
        cross[h].append(_dot(qd, st.astype(BF16)))
        upd = lax.dot_general(kd, vg[h][0][chunk_rows(c)], (((0,), (0,)), ((), ())),
                              preferred_element_type=F32)
        state_ref[h] = cdec_ref[h] * st + upd

    def retention_first(h):
        scores[h] = [lax.dot_general(qb, kb, (((1,), (1,)), ((), ())),
                                     preferred_element_type=F32)
                     for qb, _, kb, _ in roped[h]]
        cross[h] = []
        state_step(h, 0)

    def retention_rest(h):
        for c in range(1, n_chunks):
            state_step(h, c)
        roped.pop(h)

    def finish_head(h):
        vb, g = vg.pop(h)
        idec = idec_ref[h]
        outs = [_dot((s * idec).astype(BF16), vb[chunk_rows(c)]) + cr
                for c, (s, cr) in enumerate(zip(scores.pop(h), cross.pop(h)))]
        o = outs[0] if n_chunks == 1 else jnp.concatenate(outs, axis=0)
        mu = jnp.mean(o, axis=-1, keepdims=True)
        oc = o - mu
        var = jnp.mean(oc * oc, axis=-1, keepdims=True)
        ret = oc * lax.rsqrt(var + EPS) * gnw_ref[:, h * RET_DV:(h + 1) * RET_DV]
        gated_ref[:, h * RET_DV:(h + 1) * RET_DV] = (_silu(g) * ret).astype(BF16)

    def ret_branch(h):
        return _dot(gated_ref[:, h * RET_DV:(h + 1) * RET_DV],
                    w_r_ref[h * RET_DV:(h + 1) * RET_DV, :])

    def pool_windows(u):
        uext_ref[POOL_HALO:POOL_HALO + tm, :] = u
        pos = (i * tm + lax.broadcasted_iota(jnp.int32, (tm, 1), 0)).astype(F32)
        pooled = []
        for gi, w in enumerate(POOL_WINDOWS):
            cols = slice(gi * POOL_GROUP_W, (gi + 1) * POOL_GROUP_W)
            s = uext_ref[:, cols]
            span = 1
            while span < w:
                s = s + pltpu.roll(s, span, axis=0)
                span *= 2
            count = jnp.minimum(pos + 1.0, float(w))
            pooled.append((s[POOL_HALO:, :] / count - u[:, cols]).astype(BF16))
        uext_ref[0:POOL_HALO, :] = uext_ref[tm:tm + POOL_HALO, :]
        return pooled

    def pool_linear(pooled):
        for gi in range(len(POOL_WINDOWS)):
            cols = slice(gi * POOL_GROUP_W, (gi + 1) * POOL_GROUP_W)
            mixed = _dot(pooled[gi], plin_ref[cols, :])
            pmix_ref[:, cols] = (mixed * pscale_ref[:, cols]).astype(BF16)

    proj_qk(0)
    u = proj(OFF_U, POOL_W)
    proj_vg(0)
    rope_qk(0)
    pooled = pool_windows(u)
    y_r = None
    for h in range(RET_HEADS):
        last = h + 1 == RET_HEADS
        if not last:
            proj_qk(h + 1)
        retention_first(h)
        if not last:
            rope_qk(h + 1)
            proj_vg(h + 1)
        else:
            a_r = proj(OFF_AR, D_MODEL)
            a_p = proj(OFF_AP, D_MODEL)
        retention_rest(h)
        finish_head(h)
        if h == 0:
            pool_linear(pooled)
        if h >= 1:
            part = ret_branch(h - 1)
            y_r = part if y_r is None else y_r + part
    y_p = _dot(pmix_ref[...], w_p_ref[...])
    y_r = y_r + ret_branch(RET_HEADS - 1)

    half = tm // 2
    for r in range(2):
        rows = slice(r * half, (r + 1) * half)
        merged = (jax.nn.sigmoid(a_r[rows]) * y_r[rows]
                  + jax.nn.sigmoid(a_p[rows]) * y_p[rows])
        o_ref[rows, :] = x[rows] + gate * _dot(merged.astype(BF16), w_out_ref[...])


def _retention_tables():
    c = RET_CHUNK
    log_gamma = jnp.log1p(-(2.0 ** (-5.0 - jnp.arange(RET_HEADS, dtype=F32))))
    idx = jnp.arange(c, dtype=F32)
    diff = idx[:, None] - idx[None, :]
    inner = jnp.where(diff >= 0, jnp.exp(log_gamma[:, None, None] * jnp.maximum(diff, 0.0)), 0.0)
    q_decay = jnp.exp(log_gamma[:, None] * (idx + 1.0))
    k_decay = jnp.exp(log_gamma[:, None] * (c - 1.0 - idx))
    chunk_decay = jnp.exp(log_gamma * c)
    wide = (RET_HEADS, c, RET_DK)
    return (inner, jnp.broadcast_to(q_decay[:, :, None], wide),
            jnp.broadcast_to(k_decay[:, :, None], wide), chunk_decay)


def _rope_tables(seq):
    inv = 1.0 / (ROPE_BASE ** (jnp.arange(ROPE_HALF, dtype=F32) / ROPE_HALF))
    ang = jnp.arange(seq, dtype=F32)[:, None] * inv[None, :]
    return jnp.cos(ang), jnp.sin(ang)


def _mixer(x, mod, norm_w, w_in, gn_w, w_r, pool_lin, pool_scale, w_p, w_out):
    batch, seq, d = x.shape
    tm = MIX_TOKENS
    cos, sin = _rope_tables(seq)
    idec, qdec, kdec, cdec = _retention_tables()
    in_specs = [
        pl.BlockSpec((None, tm, d), lambda b, i: (b, i, 0)),
        pl.BlockSpec((None, N_MOD, d), lambda b, i: (b, 0, 0)),
        _resident((1, d)),
        pl.BlockSpec((tm, ROPE_HALF), lambda b, i: (i, 0)),
        pl.BlockSpec((tm, ROPE_HALF), lambda b, i: (i, 0)),
        _resident(idec.shape),
        _resident(qdec.shape),
        _resident(kdec.shape),
        pl.BlockSpec(memory_space=pltpu.SMEM),
        pl.BlockSpec(memory_space=pl.ANY),
        _resident((1, RET_HEADS * RET_DV)),
        pl.BlockSpec(memory_space=pl.ANY),
        pl.BlockSpec(memory_space=pl.ANY),
        _resident((1, POOL_W)),
        pl.BlockSpec(memory_space=pl.ANY),
        pl.BlockSpec(memory_space=pl.ANY),
    ]
    pool_lin = pool_lin.reshape(POOL_W, POOL_GROUP_W)
    return pl.pallas_call(
        _mixer_kernel,
        grid=(batch, seq // tm),
        in_specs=in_specs,
        out_specs=pl.BlockSpec((None, tm, d), lambda b, i: (b, i, 0)),
        out_shape=jax.ShapeDtypeStruct(x.shape, F32),
        scratch_shapes=[
            pltpu.VMEM((RET_HEADS, RET_DK, RET_DV), F32),
            pltpu.VMEM((POOL_HALO + tm, POOL_W), F32),
            pltpu.VMEM((tm, RET_HEADS * RET_DV), BF16),
            pltpu.VMEM((tm, POOL_W), BF16),
            pltpu.VMEM(w_in.shape, BF16),
            pltpu.VMEM(w_r.shape, BF16),
            pltpu.VMEM(pool_lin.shape, BF16),
            pltpu.VMEM(w_p.shape, BF16),
            pltpu.VMEM(w_out.shape, BF16),
            pltpu.VMEM((STAGE_SLOTS, MIX_STAGE_ROWS[0], w_in.shape[1]), F32),
            pltpu.VMEM((STAGE_SLOTS, MIX_STAGE_ROWS[1], d), F32),
            pltpu.VMEM((STAGE_SLOTS, MIX_STAGE_ROWS[1], POOL_GROUP_W), F32),
            pltpu.SemaphoreType.DMA((STAGE_SLOTS,)),
        ],
        compiler_params=pltpu.CompilerParams(
            dimension_semantics=("arbitrary", "arbitrary"),
            vmem_limit_bytes=V7X_VMEM_LIMIT_BYTES),
    )(x, mod, norm_w.reshape(1, d), cos, sin, idec, qdec, kdec, cdec, w_in,
      gn_w.reshape(1, -1), w_r, pool_lin, pool_scale.reshape(1, -1), w_p, w_out)


def kernel(x, c, ada_w, ada_b, norm_ffn1, ffn1_w13, ffn1_w2, norm_mix, w_in, ret_gn_w,
           w_ret_branch, pool_lin, pool_scale, w_pool_branch, w_out, norm_ffn2, ffn2_w13,
           ffn2_w2, norm_final):
    batch, _, d = x.shape
    depth = ada_w.shape[0]
    for l in range(depth):
        mod = _modulation(c, ada_w[l], ada_b[l]).reshape(batch, N_MOD, d)
        x = _ffn(x, mod, 0, norm_ffn1[l], ffn1_w13[l], ffn1_w2[l], None)
        x = _mixer(x, mod, norm_mix[l], w_in[l], ret_gn_w[l], w_ret_branch[l], pool_lin[l],
                   pool_scale[l], w_pool_branch[l], w_out[l])
        x = _ffn(x, mod, 6, norm_ffn2[l], ffn2_w13[l], ffn2_w2[l],
                 norm_final if l == depth - 1 else None)
    return x
```

```python
import functools

import jax
import jax.numpy as jnp
import numpy as np
from jax import lax
from jax.experimental import pallas as pl
from jax.experimental.pallas import tpu as pltpu

F32 = jnp.float32
BF16 = jnp.bfloat16

D_MODEL = 1024
N_MOD = 9
EPS = 1e-6
D_FF = 2816

RET_HEADS = 4
RET_DK = 256
RET_DV = 512
RET_CHUNK = 256
ROPE_BASE = 10000.0
ROPE_HALF = RET_DK // 2

POOL_WINDOWS = (2, 4, 8, 16)
POOL_GROUP_W = 256
POOL_W = len(POOL_WINDOWS) * POOL_GROUP_W
POOL_HALO = 16

OFF_Q = 0
OFF_K = OFF_Q + RET_HEADS * RET_DK
OFF_V = OFF_K + RET_HEADS * RET_DK
OFF_G = OFF_V + RET_HEADS * RET_DV
OFF_U = OFF_G + RET_HEADS * RET_DV
OFF_AR = OFF_U + POOL_W
OFF_AP = OFF_AR + D_MODEL
IN_W = OFF_AP + D_MODEL

V7X_VMEM_LIMIT_BYTES = 61 * 1024 * 1024
V7X_MXU_WIDTH = 256

FFN_TOKENS = 1024
HEAD_ROWS = 128
TAIL_BLOCKS = 2
TAIL_ROWS = 256
FFN_BLOCK = V7X_MXU_WIDTH
MIX_TOKENS = 2 * RET_CHUNK
MOD_BLOCK = 1024
STAGE_SLOTS = 4
FFN_STAGE_ROWS = (64, 256)
MIX_STAGE_ROWS = (16, 128)


def _dot(a, b):
    return jnp.dot(a, b, preferred_element_type=F32)


def _silu(a):
    return a * jax.nn.sigmoid(a)


def _rms_mod(x, norm_w, shift, scale):
    ms = jnp.mean(x * x, axis=-1, keepdims=True)
    y = x * lax.rsqrt(ms + EPS) * norm_w
    return y * (1.0 + scale) + shift


def _resident(shape):
    zeros = (0,) * len(shape)
    return pl.BlockSpec(shape, lambda *_: zeros, pipeline_mode=pl.Buffered(1))


def _is_first_step():
    return jnp.logical_and(pl.program_id(0) == 0, pl.program_id(1) == 0)


def _mod_rows(mod_ref, first):
    b = pl.program_id(0)
    return [mod_ref[pl.ds(b, 1), (first + k) * D_MODEL:(first + k + 1) * D_MODEL]
            for k in range(3)]


def _load_weight_bf16(src_ref, dst_ref, stage_ref, sem_ref):
    slots, rows = stage_ref.shape[:2]
    n = src_ref.shape[0] // rows

    def copy(c):
        return pltpu.make_async_copy(src_ref.at[pl.ds(c * rows, rows)], stage_ref.at[c % slots],
                                     sem_ref.at[c % slots])

    for c in range(min(slots, n)):
        copy(c).start()
    for c in range(n):
        copy(c).wait()
        dst_ref[c * rows:(c + 1) * rows, :] = stage_ref[c % slots].astype(BF16)
        if c + slots < n:
            copy(c + slots).start()


def _mod_kernel(c_ref, w_ref, b_ref, o_ref):
    c = c_ref[...]
    o_ref[...] = _dot(_silu(c).astype(BF16), w_ref[...].astype(BF16)) + b_ref[...]


def _modulation(c, w, b):
    batch, d = c.shape
    n = w.shape[1]
    return pl.pallas_call(
        _mod_kernel,
        grid=(n // MOD_BLOCK,),
        in_specs=[
            pl.BlockSpec((batch, d), lambda j: (0, 0)),
            pl.BlockSpec((d, MOD_BLOCK), lambda j: (0, j)),
            pl.BlockSpec((1, MOD_BLOCK), lambda j: (0, j)),
        ],
        out_specs=pl.BlockSpec((batch, MOD_BLOCK), lambda j: (0, j)),
        out_shape=jax.ShapeDtypeStruct((batch, n), F32),
        compiler_params=pltpu.CompilerParams(dimension_semantics=("arbitrary",)),
    )(c, w, b.reshape(1, n))


def _ffn_kernel(x_ref, mod_ref, nw_ref, w13_hbm, w2_hbm, *rest, mod_base, final_norm):
    o_ref, w13_ref, w2_ref, stage13_ref, stage2_ref, sem_ref = rest[-6:]

    @pl.when(_is_first_step())
    def _():
        _load_weight_bf16(w13_hbm, w13_ref, stage13_ref, sem_ref)
        _load_weight_bf16(w2_hbm, w2_ref, stage2_ref, sem_ref)

    x = x_ref[...]
    shift, scale, gate = _mod_rows(mod_ref, mod_base)
    hb = _rms_mod(x, nw_ref[...], shift, scale).astype(BF16)

    def up(lhs, lo):
        return (_dot(lhs, w13_ref[:, lo:lo + FFN_BLOCK]),
                _dot(lhs, w13_ref[:, D_FF + lo:D_FF + lo + FFN_BLOCK]))

    def down(a, b, lo):
        return _dot((_silu(a) * b).astype(BF16), w2_ref[lo:lo + FFN_BLOCK, :])

    n_blocks = D_FF // FFN_BLOCK
    tail_blocks = TAIL_BLOCKS if final_norm else 0
    acc = jnp.zeros(x.shape, F32)
    for j in range(n_blocks - tail_blocks):
        lo = j * FFN_BLOCK
        if j == 0:
            parts = [up(hb[r:r + HEAD_ROWS], lo) for r in range(0, x.shape[0], HEAD_ROWS)]
            a = jnp.concatenate([p[0] for p in parts], axis=0)
            b = jnp.concatenate([p[1] for p in parts], axis=0)
        else:
            a, b = up(hb, lo)
        acc = acc + down(a, b, lo)
    chunks = [slice(r, r + TAIL_ROWS) for r in range(0, x.shape[0], TAIL_ROWS)]
    tail = [j * FFN_BLOCK for j in range(n_blocks - tail_blocks, n_blocks)]
    ups = {(0, lo): up(hb[chunks[0]], lo) for lo in tail}
    for r, rows in enumerate(chunks):
        acc_r = acc[rows]
        for lo in tail:
            acc_r = acc_r + down(*ups.pop((r, lo)), lo)
            if r + 1 < len(chunks):
                ups[(r + 1, lo)] = up(hb[chunks[r + 1]], lo)
        out = x[rows] + gate * (0.5 * acc_r)
        if final_norm:
            ms = jnp.mean(out * out, axis=-1, keepdims=True)
            out = out * lax.rsqrt(ms + EPS) * rest[0][...]
        o_ref[rows, :] = out


def _ffn(x, mod, mod_base, norm_w, w13, w2, final_w):
    batch, seq, d = x.shape
    tm = FFN_TOKENS
    final_norm = final_w is not None
    in_specs = [
        pl.BlockSpec((None, tm, d), lambda b, i: (b, i, 0)),
        _resident(mod.shape),
        _resident((1, d)),
        pl.BlockSpec(memory_space=pl.ANY),
        pl.BlockSpec(memory_space=pl.ANY),
    ]
    args = [x, mod, norm_w.reshape(1, d), w13, w2]
    if final_norm:
        in_specs.append(_resident((1, d)))
        args.append(final_w.reshape(1, d))
    return pl.pallas_call(
        functools.partial(_ffn_kernel, mod_base=mod_base, final_norm=final_norm),
        grid=(batch, seq // tm),
        in_specs=in_specs,
        out_specs=pl.BlockSpec((None, tm, d), lambda b, i: (b, i, 0)),
        out_shape=jax.ShapeDtypeStruct(x.shape, F32),
        scratch_shapes=[
            pltpu.VMEM(w13.shape, BF16),
            pltpu.VMEM(w2.shape, BF16),
            pltpu.VMEM((STAGE_SLOTS, FFN_STAGE_ROWS[0], w13.shape[1]), F32),
            pltpu.VMEM((STAGE_SLOTS, FFN_STAGE_ROWS[1], w2.shape[1]), F32),
            pltpu.SemaphoreType.DMA((STAGE_SLOTS,)),
        ],
        compiler_params=pltpu.CompilerParams(
            dimension_semantics=("arbitrary", "arbitrary"),
            vmem_limit_bytes=V7X_VMEM_LIMIT_BYTES),
    )(*args)


def _rope(t, cos, sin):
    t1, t2 = t[:, :ROPE_HALF], t[:, ROPE_HALF:]
    return jnp.concatenate([t1 * cos - t2 * sin, t1 * sin + t2 * cos], axis=-1)


def _mixer_kernel(x_ref, mod_ref, nw_ref, cos_ref, sin_ref, idec_ref, qdec_ref, kdec_ref,
                  cdec_ref, w_in_hbm, gnw_ref, w_r_hbm, plin_hbm, pscale_ref, w_p_hbm,
                  w_out_hbm, o_ref, state_ref, uext_ref, gated_ref,
                  w_in_ref, w_r_ref, plin_ref, w_p_ref, w_out_ref,
                  stage_in_ref, stage_ref, stage_plin_ref, sem_ref):
    i = pl.program_id(1)
    tm = x_ref.shape[0]
    n_chunks = tm // RET_CHUNK

    @pl.when(_is_first_step())
    def _():
        _load_weight_bf16(w_in_hbm, w_in_ref, stage_in_ref, sem_ref)
        _load_weight_bf16(w_r_hbm, w_r_ref, stage_ref, sem_ref)
        _load_weight_bf16(w_p_hbm, w_p_ref, stage_ref, sem_ref)
        _load_weight_bf16(w_out_hbm, w_out_ref, stage_ref, sem_ref)
        _load_weight_bf16(plin_hbm, plin_ref, stage_plin_ref, sem_ref)
        for gi in range(len(POOL_WINDOWS)):
            cols = slice(gi * POOL_GROUP_W, (gi + 1) * POOL_GROUP_W)
            lin = plin_ref[cols, :].astype(F32) * pscale_ref[:, cols]
            hi = lin.astype(BF16)
            lo = (lin - hi.astype(F32)).astype(BF16)
            w_g = w_p_ref[cols, :]
            w_p_ref[cols, :] = (_dot(hi, w_g) + _dot(lo, w_g)).astype(BF16)

    @pl.when(i == 0)
    def _():
        state_ref[...] = jnp.zeros(state_ref.shape, F32)
        uext_ref[0:POOL_HALO, :] = jnp.zeros((POOL_HALO, POOL_W), F32)

    x = x_ref[...]
    shift, scale, gate = _mod_rows(mod_ref, 3)
    hb = _rms_mod(x, nw_ref[...], shift, scale).astype(BF16)
    cos = cos_ref[...]
    sin = sin_ref[...]

    def proj(off, width):
        return _dot(hb, w_in_ref[:, off:off + width])

    qk, roped, vg, scores, cross = {}, {}, {}, {}, {}

    def proj_qk(h):
        qk[h] = (proj(OFF_Q + h * RET_DK, RET_DK), proj(OFF_K + h * RET_DK, RET_DK))

    def chunk_rows(c):
        return slice(c * RET_CHUNK, (c + 1) * RET_CHUNK)

    def rope_qk(h):
        q, k = qk.pop(h)
        qr = _rope(q, cos, sin)
        kr = _rope(k, cos, sin) * (RET_DK ** -0.5)
        qdec, kdec = qdec_ref[h], kdec_ref[h]
        roped[h] = [(qr[chunk_rows(c)].astype(BF16), (qr[chunk_rows(c)] * qdec).astype(BF16),
                     kr[chunk_rows(c)].astype(BF16), (kr[chunk_rows(c)] * kdec).astype(BF16))
                    for c in range(n_chunks)]

    def proj_vg(h):
        v = proj(OFF_V + h * RET_DV, RET_DV)
        vg[h] = (v.astype(BF16), proj(OFF_G + h * RET_DV, RET_DV))

    def state_step(h, c):
        _, qd, _, kd = roped[h][c]
        st = state_ref[h]
        cross[h].append(_dot(qd, st.astype(BF16)))
        upd = lax.dot_general(kd, vg[h][0][chunk_rows(c)], (((0,), (0,)), ((), ())),
                              preferred_element_type=F32)
        state_ref[h] = cdec_ref[h] * st + upd

    def retention_first(h):
        scores[h] = [lax.dot_general(qb, kb, (((1,), (1,)), ((), ())),
                                     preferred_element_type=F32)
                     for qb, _, kb, _ in roped[h]]
        cross[h] = []
        state_step(h, 0)

    def retention_rest(h):
        for c in range(1, n_chunks):
            state_step(h, c)
        roped.pop(h)

    def finish_head(h):
        vb, g = vg.pop(h)
        idec = idec_ref[h]
        outs = [_dot((s * idec).astype(BF16), vb[chunk_rows(c)]) + cr
                for c, (s, cr) in enumerate(zip(scores.pop(h), cross.pop(h)))]
        o = outs[0] if n_chunks == 1 else jnp.concatenate(outs, axis=0)
        mu = jnp.mean(o, axis=-1, keepdims=True)
        oc = o - mu
        var = jnp.mean(oc * oc, axis=-1, keepdims=True)
        ret = oc * lax.rsqrt(var + EPS) * gnw_ref[:, h * RET_DV:(h + 1) * RET_DV]
        gated_ref[:, h * RET_DV:(h + 1) * RET_DV] = (_silu(g) * ret).astype(BF16)

    def ret_branch(h):
        return _dot(gated_ref[:, h * RET_DV:(h + 1) * RET_DV],
                    w_r_ref[h * RET_DV:(h + 1) * RET_DV, :])

    def pool_windows(u):
        uext_ref[POOL_HALO:POOL_HALO + tm, :] = u
        pos = (i * tm + lax.broadcasted_iota(jnp.int32, (tm, 1), 0)).astype(F32)
        pooled = []
        for gi, w in enumerate(POOL_WINDOWS):
            cols = slice(gi * POOL_GROUP_W, (gi + 1) * POOL_GROUP_W)
            s = uext_ref[:, cols]
            span = 1
            while span < w:
                s = s + pltpu.roll(s, span, axis=0)
                span *= 2
            count = jnp.minimum(pos + 1.0, float(w))
            pooled.append((s[POOL_HALO:, :] / count - u[:, cols]).astype(BF16))
        uext_ref[0:POOL_HALO, :] = uext_ref[tm:tm + POOL_HALO, :]
        return jnp.concatenate(pooled, axis=1)

    proj_qk(0)
    u = proj(OFF_U, POOL_W)
    proj_vg(0)
    rope_qk(0)
    pooled = pool_windows(u)
    y_r = None
    for h in range(RET_HEADS):
        last = h + 1 == RET_HEADS
        if not last:
            proj_qk(h + 1)
        retention_first(h)
        if not last:
            rope_qk(h + 1)
            proj_vg(h + 1)
        else:
            a_r = proj(OFF_AR, D_MODEL)
            a_p = proj(OFF_AP, D_MODEL)
        retention_rest(h)
        finish_head(h)
        if h == 0:
            y_p = _dot(pooled, w_p_ref[...])
        if h >= 1:
            part = ret_branch(h - 1)
            y_r = part if y_r is None else y_r + part
    y_r = y_r + ret_branch(RET_HEADS - 1)

    half = tm // 2
    for r in range(2):
        rows = slice(r * half, (r + 1) * half)
        merged = (jax.nn.sigmoid(a_r[rows]) * y_r[rows]
                  + jax.nn.sigmoid(a_p[rows]) * y_p[rows])
        o_ref[rows, :] = x[rows] + gate * _dot(merged.astype(BF16), w_out_ref[...])


def _retention_tables():
    c = RET_CHUNK
    log_gamma = np.log1p(-(2.0 ** (-5.0 - np.arange(RET_HEADS, dtype=np.float32)))).astype(np.float32)
    idx = np.arange(c, dtype=np.float32)
    diff = idx[:, None] - idx[None, :]
    inner = np.where(diff >= 0, np.exp(log_gamma[:, None, None] * np.maximum(diff, 0.0)), 0.0)
    q_decay = np.exp(log_gamma[:, None] * (idx + 1.0))
    k_decay = np.exp(log_gamma[:, None] * (c - 1.0 - idx))
    chunk_decay = np.exp(log_gamma * c)
    wide = (RET_HEADS, c, RET_DK)
    return (inner.astype(np.float32),
            np.ascontiguousarray(np.broadcast_to(q_decay[:, :, None], wide), dtype=np.float32),
            np.ascontiguousarray(np.broadcast_to(k_decay[:, :, None], wide), dtype=np.float32),
            chunk_decay.astype(np.float32))


def _rope_tables(seq):
    inv = (1.0 / (ROPE_BASE ** (np.arange(ROPE_HALF, dtype=np.float32) / ROPE_HALF))).astype(np.float32)
    ang = np.arange(seq, dtype=np.float32)[:, None] * inv[None, :]
    return np.cos(ang).astype(np.float32), np.sin(ang).astype(np.float32)


def _mixer(x, mod, norm_w, w_in, gn_w, w_r, pool_lin, pool_scale, w_p, w_out):
    batch, seq, d = x.shape
    tm = MIX_TOKENS
    cos, sin = _rope_tables(seq)
    idec, qdec, kdec, cdec = _retention_tables()
    in_specs = [
        pl.BlockSpec((None, tm, d), lambda b, i: (b, i, 0)),
        _resident(mod.shape),
        _resident((1, d)),
        pl.BlockSpec((tm, ROPE_HALF), lambda b, i: (i, 0)),
        pl.BlockSpec((tm, ROPE_HALF), lambda b, i: (i, 0)),
        _resident(idec.shape),
        _resident(qdec.shape),
        _resident(kdec.shape),
        pl.BlockSpec(memory_space=pltpu.SMEM),
        pl.BlockSpec(memory_space=pl.ANY),
        _resident((1, RET_HEADS * RET_DV)),
        pl.BlockSpec(memory_space=pl.ANY),
        pl.BlockSpec(memory_space=pl.ANY),
        _resident((1, POOL_W)),
        pl.BlockSpec(memory_space=pl.ANY),
        pl.BlockSpec(memory_space=pl.ANY),
    ]
    pool_lin = pool_lin.reshape(POOL_W, POOL_GROUP_W)
    return pl.pallas_call(
        _mixer_kernel,
        grid=(batch, seq // tm),
        in_specs=in_specs,
        out_specs=pl.BlockSpec((None, tm, d), lambda b, i: (b, i, 0)),
        out_shape=jax.ShapeDtypeStruct(x.shape, F32),
        scratch_shapes=[
            pltpu.VMEM((RET_HEADS, RET_DK, RET_DV), F32),
            pltpu.VMEM((POOL_HALO + tm, POOL_W), F32),
            pltpu.VMEM((tm, RET_HEADS * RET_DV), BF16),
            pltpu.VMEM(w_in.shape, BF16),
            pltpu.VMEM(w_r.shape, BF16),
            pltpu.VMEM(pool_lin.shape, BF16),
            pltpu.VMEM(w_p.shape, BF16),
            pltpu.VMEM(w_out.shape, BF16),
            pltpu.VMEM((STAGE_SLOTS, MIX_STAGE_ROWS[0], w_in.shape[1]), F32),
            pltpu.VMEM((STAGE_SLOTS, MIX_STAGE_ROWS[1], d), F32),
            pltpu.VMEM((STAGE_SLOTS, MIX_STAGE_ROWS[1], POOL_GROUP_W), F32),
            pltpu.SemaphoreType.DMA((STAGE_SLOTS,)),
        ],
        compiler_params=pltpu.CompilerParams(
            dimension_semantics=("arbitrary", "arbitrary"),
            vmem_limit_bytes=V7X_VMEM_LIMIT_BYTES),
    )(x, mod, norm_w.reshape(1, d), cos, sin, idec, qdec, kdec, cdec, w_in,
      gn_w.reshape(1, -1), w_r, pool_lin, pool_scale.reshape(1, -1), w_p, w_out)


def kernel(x, c, ada_w, ada_b, norm_ffn1, ffn1_w13, ffn1_w2, norm_mix, w_in, ret_gn_w,
           w_ret_branch, pool_lin, pool_scale, w_pool_branch, w_out, norm_ffn2, ffn2_w13,
           ffn2_w2, norm_final):
    depth = ada_w.shape[0]
    for l in range(depth):
        mod = _modulation(c, ada_w[l], ada_b[l])
        x = _ffn(x, mod, 0, norm_ffn1[l], ffn1_w13[l], ffn1_w2[l], None)
        x = _mixer(x, mod, norm_mix[l], w_in[l], ret_gn_w[l], w_ret_branch[l], pool_lin[l],
                   pool_scale[l], w_pool_branch[l], w_out[l])
        x = _ffn(x, mod, 6, norm_ffn2[l], ffn2_w13[l], ffn2_w2[l],
                 norm_final if l == depth - 1 else None)
    return x
```

```python
import functools

import jax
import jax.numpy as jnp
import numpy as np
from jax import lax
from jax.experimental import pallas as pl
from jax.experimental.pallas import tpu as pltpu

F32 = jnp.float32
BF16 = jnp.bfloat16

D_MODEL = 1024
N_MOD = 9
EPS = 1e-6
D_FF = 2816

RET_HEADS = 4
RET_DK = 256
RET_DV = 512
RET_CHUNK = 256
ROPE_BASE = 10000.0
ROPE_HALF = RET_DK // 2

POOL_WINDOWS = (2, 4, 8, 16)
POOL_GROUP_W = 256
POOL_W = len(POOL_WINDOWS) * POOL_GROUP_W
POOL_HALO = 16

OFF_Q = 0
OFF_K = OFF_Q + RET_HEADS * RET_DK
OFF_V = OFF_K + RET_HEADS * RET_DK
OFF_G = OFF_V + RET_HEADS * RET_DV
OFF_U = OFF_G + RET_HEADS * RET_DV
OFF_AR = OFF_U + POOL_W
OFF_AP = OFF_AR + D_MODEL
IN_W = OFF_AP + D_MODEL

V7X_VMEM_LIMIT_BYTES = 61 * 1024 * 1024
V7X_MXU_WIDTH = 256

FFN_TOKENS = 1024
HEAD_ROWS = 128
TAIL_BLOCKS = 2
TAIL_ROWS = 256
FFN_BLOCK = V7X_MXU_WIDTH
MIX_TOKENS = RET_CHUNK
MOD_BLOCK = 1024
STAGE_SLOTS = 4
FFN_STAGE_ROWS = (64, 256)
MIX_STAGE_ROWS = (64, 256)


def _dot(a, b):
    return jnp.dot(a, b, preferred_element_type=F32)


def _silu(a):
    return a * jax.nn.sigmoid(a)


def _rms_mod(x, norm_w, shift, scale):
    ms = jnp.mean(x * x, axis=-1, keepdims=True)
    y = x * lax.rsqrt(ms + EPS) * norm_w
    return y * (1.0 + scale) + shift


def _resident(shape):
    zeros = (0,) * len(shape)
    return pl.BlockSpec(shape, lambda *_: zeros, pipeline_mode=pl.Buffered(1))


def _is_first_step():
    return jnp.logical_and(pl.program_id(0) == 0, pl.program_id(1) == 0)


def _mod_rows(mod_ref, first):
    b = pl.program_id(0)
    return [mod_ref[pl.ds(b, 1), (first + k) * D_MODEL:(first + k + 1) * D_MODEL]
            for k in range(3)]


def _load_weight_bf16(src_ref, dst_ref, stage_ref, sem_ref):
    slots, rows = stage_ref.shape[:2]
    n = src_ref.shape[0] // rows

    def copy(c):
        return pltpu.make_async_copy(src_ref.at[pl.ds(c * rows, rows)], stage_ref.at[c % slots],
                                     sem_ref.at[c % slots])

    for c in range(min(slots, n)):
        copy(c).start()
    for c in range(n):
        copy(c).wait()
        dst_ref[c * rows:(c + 1) * rows, :] = stage_ref[c % slots].astype(BF16)
        if c + slots < n:
            copy(c + slots).start()


def _mod_kernel(c_ref, w_ref, b_ref, o_ref):
    c = c_ref[...]
    o_ref[...] = _dot(_silu(c).astype(BF16), w_ref[...].astype(BF16)) + b_ref[...]


def _modulation(c, w, b):
    batch, d = c.shape
    n = w.shape[1]
    return pl.pallas_call(
        _mod_kernel,
        grid=(n // MOD_BLOCK,),
        in_specs=[
            pl.BlockSpec((batch, d), lambda j: (0, 0)),
            pl.BlockSpec((d, MOD_BLOCK), lambda j: (0, j)),
            pl.BlockSpec((1, MOD_BLOCK), lambda j: (0, j)),
        ],
        out_specs=pl.BlockSpec((batch, MOD_BLOCK), lambda j: (0, j)),
        out_shape=jax.ShapeDtypeStruct((batch, n), F32),
        compiler_params=pltpu.CompilerParams(dimension_semantics=("arbitrary",)),
    )(c, w, b.reshape(1, n))


def _ffn_kernel(x_ref, mod_ref, nw_ref, w13_hbm, w2_hbm, *rest, mod_base, final_norm):
    o_ref, w13_ref, w2_ref, stage13_ref, stage2_ref, sem_ref = rest[-6:]

    @pl.when(_is_first_step())
    def _():
        _load_weight_bf16(w13_hbm, w13_ref, stage13_ref, sem_ref)
        _load_weight_bf16(w2_hbm, w2_ref, stage2_ref, sem_ref)

    x = x_ref[...]
    shift, scale, gate = _mod_rows(mod_ref, mod_base)
    hb = _rms_mod(x, nw_ref[...], shift, scale).astype(BF16)

    def up(lhs, lo):
        return (_dot(lhs, w13_ref[:, lo:lo + FFN_BLOCK]),
                _dot(lhs, w13_ref[:, D_FF + lo:D_FF + lo + FFN_BLOCK]))

    def down(a, b, lo):
        return _dot((_silu(a) * b).astype(BF16), w2_ref[lo:lo + FFN_BLOCK, :])

    n_blocks = D_FF // FFN_BLOCK
    tail_blocks = TAIL_BLOCKS if final_norm else 0
    acc = jnp.zeros(x.shape, F32)
    for j in range(n_blocks - tail_blocks):
        lo = j * FFN_BLOCK
        if j == 0:
            parts = [up(hb[r:r + HEAD_ROWS], lo) for r in range(0, x.shape[0], HEAD_ROWS)]
            a = jnp.concatenate([p[0] for p in parts], axis=0)
            b = jnp.concatenate([p[1] for p in parts], axis=0)
        else:
            a, b = up(hb, lo)
        acc = acc + down(a, b, lo)
    chunks = [slice(r, r + TAIL_ROWS) for r in range(0, x.shape[0], TAIL_ROWS)]
    tail = [j * FFN_BLOCK for j in range(n_blocks - tail_blocks, n_blocks)]
    ups = {(0, lo): up(hb[chunks[0]], lo) for lo in tail}
    for r, rows in enumerate(chunks):
        acc_r = acc[rows]
        for lo in tail:
            acc_r = acc_r + down(*ups.pop((r, lo)), lo)
            if r + 1 < len(chunks):
                ups[(r + 1, lo)] = up(hb[chunks[r + 1]], lo)
        out = x[rows] + gate * (0.5 * acc_r)
        if final_norm:
            ms = jnp.mean(out * out, axis=-1, keepdims=True)
            out = out * lax.rsqrt(ms + EPS) * rest[0][...]
        o_ref[rows, :] = out


def _ffn(x, mod, mod_base, norm_w, w13, w2, final_w):
    batch, seq, d = x.shape
    tm = FFN_TOKENS
    final_norm = final_w is not None
    in_specs = [
        pl.BlockSpec((None, tm, d), lambda b, i: (b, i, 0)),
        _resident(mod.shape),
        _resident((1, d)),
        pl.BlockSpec(memory_space=pl.ANY),
        pl.BlockSpec(memory_space=pl.ANY),
    ]
    args = [x, mod, norm_w.reshape(1, d), w13, w2]
    if final_norm:
        in_specs.append(_resident((1, d)))
        args.append(final_w.reshape(1, d))
    return pl.pallas_call(
        functools.partial(_ffn_kernel, mod_base=mod_base, final_norm=final_norm),
        grid=(batch, seq // tm),
        in_specs=in_specs,
        out_specs=pl.BlockSpec((None, tm, d), lambda b, i: (b, i, 0)),
        out_shape=jax.ShapeDtypeStruct(x.shape, F32),
        scratch_shapes=[
            pltpu.VMEM(w13.shape, BF16),
            pltpu.VMEM(w2.shape, BF16),
            pltpu.VMEM((STAGE_SLOTS, FFN_STAGE_ROWS[0], w13.shape[1]), F32),
            pltpu.VMEM((STAGE_SLOTS, FFN_STAGE_ROWS[1], w2.shape[1]), F32),
            pltpu.SemaphoreType.DMA((STAGE_SLOTS,)),
        ],
        compiler_params=pltpu.CompilerParams(
            dimension_semantics=("arbitrary", "arbitrary"),
            vmem_limit_bytes=V7X_VMEM_LIMIT_BYTES),
    )(*args)


def _rope(t, cos, sin):
    t1, t2 = t[:, :ROPE_HALF], t[:, ROPE_HALF:]
    return jnp.concatenate([t1 * cos - t2 * sin, t1 * sin + t2 * cos], axis=-1)


def _mixer_kernel(x_ref, mod_ref, nw_ref, cos_ref, sin_ref, idec_ref, qdec_ref, kdec_ref,
                  cdec_ref, w_in_hbm, gnw_ref, w_r_hbm, plin_hbm, pscale_ref, w_p_hbm,
                  w_out_hbm, o_ref, state_ref, uext_ref, gated_ref,
                  w_in_ref, w_r_ref, plin_ref, w_p_ref, w_out_ref,
                  stage_in_ref, stage_ref, stage_plin_ref, sem_ref):
    i = pl.program_id(1)
    tm = x_ref.shape[0]
    n_chunks = tm // RET_CHUNK

    @pl.when(_is_first_step())
    def _():
        _load_weight_bf16(w_in_hbm, w_in_ref, stage_in_ref, sem_ref)
        _load_weight_bf16(w_r_hbm, w_r_ref, stage_ref, sem_ref)
        _load_weight_bf16(w_p_hbm, w_p_ref, stage_ref, sem_ref)
        _load_weight_bf16(w_out_hbm, w_out_ref, stage_ref, sem_ref)
        _load_weight_bf16(plin_hbm, plin_ref, stage_plin_ref, sem_ref)
        for gi in range(len(POOL_WINDOWS)):
            cols = slice(gi * POOL_GROUP_W, (gi + 1) * POOL_GROUP_W)
            lin = plin_ref[cols, :].astype(F32) * pscale_ref[:, cols]
            hi = lin.astype(BF16)
            lo = (lin - hi.astype(F32)).astype(BF16)
            w_g = w_p_ref[cols, :]
            w_p_ref[cols, :] = (_dot(hi, w_g) + _dot(lo, w_g)).astype(BF16)

    @pl.when(i == 0)
    def _():
        state_ref[...] = jnp.zeros(state_ref.shape, F32)
        uext_ref[0:POOL_HALO, :] = jnp.zeros((POOL_HALO, POOL_W), F32)

    x = x_ref[...]
    shift, scale, gate = _mod_rows(mod_ref, 3)
    hb = _rms_mod(x, nw_ref[...], shift, scale).astype(BF16)
    cos = cos_ref[...]
    sin = sin_ref[...]

    def proj(off, width):
        return _dot(hb, w_in_ref[:, off:off + width])

    qk, roped, vg, scores, cross = {}, {}, {}, {}, {}

    def proj_qk(h):
        qk[h] = (proj(OFF_Q + h * RET_DK, RET_DK), proj(OFF_K + h * RET_DK, RET_DK))

    def chunk_rows(c):
        return slice(c * RET_CHUNK, (c + 1) * RET_CHUNK)

    def rope_qk(h):
        q, k = qk.pop(h)
        qr = _rope(q, cos, sin)
        kr = _rope(k, cos, sin) * (RET_DK ** -0.5)
        qdec, kdec = qdec_ref[h], kdec_ref[h]
        roped[h] = [(qr[chunk_rows(c)].astype(BF16), (qr[chunk_rows(c)] * qdec).astype(BF16),
                     kr[chunk_rows(c)].astype(BF16), (kr[chunk_rows(c)] * kdec).astype(BF16))
                    for c in range(n_chunks)]

    def proj_vg(h):
        v = proj(OFF_V + h * RET_DV, RET_DV)
        vg[h] = (v.astype(BF16), proj(OFF_G + h * RET_DV, RET_DV))

    def state_step(h, c):
        _, qd, _, kd = roped[h][c]
        st = state_ref[h]
        cross[h].append(_dot(qd, st.astype(BF16)))
        upd = lax.dot_general(kd, vg[h][0][chunk_rows(c)], (((0,), (0,)), ((), ())),
                              preferred_element_type=F32)
        state_ref[h] = cdec_ref[h] * st + upd

    def retention_first(h):
        scores[h] = [lax.dot_general(qb, kb, (((1,), (1,)), ((), ())),
                                     preferred_element_type=F32)
                     for qb, _, kb, _ in roped[h]]
        cross[h] = []
        state_step(h, 0)

    def retention_rest(h):
        for c in range(1, n_chunks):
            state_step(h, c)
        roped.pop(h)

    def finish_head(h):
        vb, g = vg.pop(h)
        idec = idec_ref[h]
        outs = [_dot((s * idec).astype(BF16), vb[chunk_rows(c)]) + cr
                for c, (s, cr) in enumerate(zip(scores.pop(h), cross.pop(h)))]
        o = outs[0] if n_chunks == 1 else jnp.concatenate(outs, axis=0)
        mu = jnp.mean(o, axis=-1, keepdims=True)
        oc = o - mu
        var = jnp.mean(oc * oc, axis=-1, keepdims=True)
        ret = oc * lax.rsqrt(var + EPS) * gnw_ref[:, h * RET_DV:(h + 1) * RET_DV]
        gated_ref[:, h * RET_DV:(h + 1) * RET_DV] = (_silu(g) * ret).astype(BF16)

    def ret_branch(h):
        return _dot(gated_ref[:, h * RET_DV:(h + 1) * RET_DV],
                    w_r_ref[h * RET_DV:(h + 1) * RET_DV, :])

    def pool_windows(u):
        uext_ref[POOL_HALO:POOL_HALO + tm, :] = u
        pos = (i * tm + lax.broadcasted_iota(jnp.int32, (tm, 1), 0)).astype(F32)
        pooled = []
        for gi, w in enumerate(POOL_WINDOWS):
            cols = slice(gi * POOL_GROUP_W, (gi + 1) * POOL_GROUP_W)
            s = uext_ref[:, cols]
            span = 1
            while span < w:
                s = s + pltpu.roll(s, span, axis=0)
                span *= 2
            count = jnp.minimum(pos + 1.0, float(w))
            pooled.append((s[POOL_HALO:, :] / count - u[:, cols]).astype(BF16))
        uext_ref[0:POOL_HALO, :] = uext_ref[tm:tm + POOL_HALO, :]
        return jnp.concatenate(pooled, axis=1)

    proj_qk(0)
    u = proj(OFF_U, POOL_W)
    proj_vg(0)
    rope_qk(0)
    pooled = pool_windows(u)
    y_r = None
    for h in range(RET_HEADS):
        last = h + 1 == RET_HEADS
        if not last:
            proj_qk(h + 1)
        retention_first(h)
        if not last:
            rope_qk(h + 1)
            proj_vg(h + 1)
        else:
            a_r = proj(OFF_AR, D_MODEL)
            a_p = proj(OFF_AP, D_MODEL)
        retention_rest(h)
        finish_head(h)
        if h == 0:
            y_p = _dot(pooled, w_p_ref[...])
        if h >= 1:
            part = ret_branch(h - 1)
            y_r = part if y_r is None else y_r + part
    y_r = y_r + ret_branch(RET_HEADS - 1)

    half = tm // 2
    for r in range(2):
        rows = slice(r * half, (r + 1) * half)
        merged = (jax.nn.sigmoid(a_r[rows]) * y_r[rows]
                  + jax.nn.sigmoid(a_p[rows]) * y_p[rows])
        o_ref[rows, :] = x[rows] + gate * _dot(merged.astype(BF16), w_out_ref[...])


def _retention_tables():
    c = RET_CHUNK
    log_gamma = np.log1p(-(2.0 ** (-5.0 - np.arange(RET_HEADS, dtype=np.float32)))).astype(np.float32)
    idx = np.arange(c, dtype=np.float32)
    diff = idx[:, None] - idx[None, :]
    inner = np.where(diff >= 0, np.exp(log_gamma[:, None, None] * np.maximum(diff, 0.0)), 0.0)
    q_decay = np.exp(log_gamma[:, None] * (idx + 1.0))
    k_decay = np.exp(log_gamma[:, None] * (c - 1.0 - idx))
    chunk_decay = np.exp(log_gamma * c)
    wide = (RET_HEADS, c, RET_DK)
    return (inner.astype(np.float32),
            np.ascontiguousarray(np.broadcast_to(q_decay[:, :, None], wide), dtype=np.float32),
            np.ascontiguousarray(np.broadcast_to(k_decay[:, :, None], wide), dtype=np.float32),
            chunk_decay.astype(np.float32))


def _rope_tables(seq):
    inv = (1.0 / (ROPE_BASE ** (np.arange(ROPE_HALF, dtype=np.float32) / ROPE_HALF))).astype(np.float32)
    ang = np.arange(seq, dtype=np.float32)[:, None] * inv[None, :]
    return np.cos(ang).astype(np.float32), np.sin(ang).astype(np.float32)


def _mixer(x, mod, norm_w, w_in, gn_w, w_r, pool_lin, pool_scale, w_p, w_out):
    batch, seq, d = x.shape
    tm = MIX_TOKENS
    cos, sin = _rope_tables(seq)
    idec, qdec, kdec, cdec = _retention_tables()
    in_specs = [
        pl.BlockSpec((None, tm, d), lambda b, i: (b, i, 0)),
        _resident(mod.shape),
        _resident((1, d)),
        pl.BlockSpec((tm, ROPE_HALF), lambda b, i: (i, 0)),
        pl.BlockSpec((tm, ROPE_HALF), lambda b, i: (i, 0)),
        _resident(idec.shape),
        _resident(qdec.shape),
        _resident(kdec.shape),
        pl.BlockSpec(memory_space=pltpu.SMEM),
        pl.BlockSpec(memory_space=pl.ANY),
        _resident((1, RET_HEADS * RET_DV)),
        pl.BlockSpec(memory_space=pl.ANY),
        pl.BlockSpec(memory_space=pl.ANY),
        _resident((1, POOL_W)),
        pl.BlockSpec(memory_space=pl.ANY),
        pl.BlockSpec(memory_space=pl.ANY),
    ]
    pool_lin = pool_lin.reshape(POOL_W, POOL_GROUP_W)
    return pl.pallas_call(
        _mixer_kernel,
        grid=(batch, seq // tm),
        in_specs=in_specs,
        out_specs=pl.BlockSpec((None, tm, d), lambda b, i: (b, i, 0)),
        out_shape=jax.ShapeDtypeStruct(x.shape, F32),
        scratch_shapes=[
            pltpu.VMEM((RET_HEADS, RET_DK, RET_DV), F32),
            pltpu.VMEM((POOL_HALO + tm, POOL_W), F32),
            pltpu.VMEM((tm, RET_HEADS * RET_DV), BF16),
            pltpu.VMEM(w_in.shape, BF16),
            pltpu.VMEM(w_r.shape, BF16),
            pltpu.VMEM(pool_lin.shape, BF16),
            pltpu.VMEM(w_p.shape, BF16),
            pltpu.VMEM(w_out.shape, BF16),
            pltpu.VMEM((STAGE_SLOTS, MIX_STAGE_ROWS[0], w_in.shape[1]), F32),
            pltpu.VMEM((STAGE_SLOTS, MIX_STAGE_ROWS[1], d), F32),
            pltpu.VMEM((STAGE_SLOTS, MIX_STAGE_ROWS[1], POOL_GROUP_W), F32),
            pltpu.SemaphoreType.DMA((STAGE_SLOTS,)),
        ],
        compiler_params=pltpu.CompilerParams(
            dimension_semantics=("arbitrary", "arbitrary"),
            vmem_limit_bytes=V7X_VMEM_LIMIT_BYTES),
    )(x, mod, norm_w.reshape(1, d), cos, sin, idec, qdec, kdec, cdec, w_in,
      gn_w.reshape(1, -1), w_r, pool_lin, pool_scale.reshape(1, -1), w_p, w_out)


def kernel(x, c, ada_w, ada_b, norm_ffn1, ffn1_w13, ffn1_w2, norm_mix, w_in, ret_gn_w,
           w_ret_branch, pool_lin, pool_scale, w_pool_branch, w_out, norm_ffn2, ffn2_w13,
           ffn2_w2, norm_final):
    depth = ada_w.shape[0]
    for l in range(depth):
        mod = _modulation(c, ada_w[l], ada_b[l])
        x = _ffn(x, mod, 0, norm_ffn1[l], ffn1_w13[l], ffn1_w2[l], None)
        x = _mixer(x, mod, norm_mix[l], w_in[l], ret_gn_w[l], w_ret_branch[l], pool_lin[l],
                   pool_scale[l], w_pool_branch[l], w_out[l])
        x = _ffn(x, mod, 6, norm_ffn2[l], ffn2_w13[l], ffn2_w2[l],
                 norm_final if l == depth - 1 else None)
    return x
```

```python
import functools

import jax
import jax.numpy as jnp
import numpy as np
from jax import lax
from jax.experimental import pallas as pl
from jax.experimental.pallas import tpu as pltpu

F32 = jnp.float32
BF16 = jnp.bfloat16

D_MODEL = 1024
N_MOD = 9
EPS = 1e-6
D_FF = 2816

RET_HEADS = 4
RET_DK = 256
RET_DV = 512
RET_CHUNK = 256
ROPE_BASE = 10000.0
ROPE_HALF = RET_DK // 2

POOL_WINDOWS = (2, 4, 8, 16)
POOL_GROUP_W = 256
POOL_W = len(POOL_WINDOWS) * POOL_GROUP_W
POOL_HALO = 16

OFF_Q = 0
OFF_K = OFF_Q + RET_HEADS * RET_DK
OFF_V = OFF_K + RET_HEADS * RET_DK
OFF_G = OFF_V + RET_HEADS * RET_DV
OFF_U = OFF_G + RET_HEADS * RET_DV
OFF_AR = OFF_U + POOL_W
OFF_AP = OFF_AR + D_MODEL
IN_W = OFF_AP + D_MODEL

V7X_VMEM_LIMIT_BYTES = 61 * 1024 * 1024
V7X_MXU_WIDTH = 256

FFN_TOKENS = 1024
HEAD_ROWS = 128
TAIL_ROWS = 256
FFN_BLOCK = V7X_MXU_WIDTH
MIX_TOKENS = 2 * RET_CHUNK
MOD_BLOCK = 1024
STAGE_SLOTS = 4
FFN_STAGE_ROWS = (64, 256)
MIX_STAGE_ROWS = (16, 128)


def _dot(a, b):
    return jnp.dot(a, b, preferred_element_type=F32)


def _silu(a):
    return a * jax.nn.sigmoid(a)


def _rms_mod(x, norm_w, shift, scale):
    ms = jnp.mean(x * x, axis=-1, keepdims=True)
    y = x * lax.rsqrt(ms + EPS) * norm_w
    return y * (1.0 + scale) + shift


def _resident(shape):
    zeros = (0,) * len(shape)
    return pl.BlockSpec(shape, lambda *_: zeros, pipeline_mode=pl.Buffered(1))


def _is_first_step():
    return jnp.logical_and(pl.program_id(0) == 0, pl.program_id(1) == 0)


def _mod_rows(mod_ref, first):
    b = pl.program_id(0)
    return [mod_ref[pl.ds(b, 1), (first + k) * D_MODEL:(first + k + 1) * D_MODEL]
            for k in range(3)]


def _load_weight_bf16(src_ref, dst_ref, stage_ref, sem_ref):
    slots, rows = stage_ref.shape[:2]
    n = src_ref.shape[0] // rows

    def copy(c):
        return pltpu.make_async_copy(src_ref.at[pl.ds(c * rows, rows)], stage_ref.at[c % slots],
                                     sem_ref.at[c % slots])

    for c in range(min(slots, n)):
        copy(c).start()
    for c in range(n):
        copy(c).wait()
        dst_ref[c * rows:(c + 1) * rows, :] = stage_ref[c % slots].astype(BF16)
        if c + slots < n:
            copy(c + slots).start()


def _mod_kernel(c_ref, w_ref, b_ref, o_ref):
    c = c_ref[...]
    o_ref[...] = _dot(_silu(c).astype(BF16), w_ref[...].astype(BF16)) + b_ref[...]


def _modulation(c, w, b):
    batch, d = c.shape
    n = w.shape[1]
    return pl.pallas_call(
        _mod_kernel,
        grid=(n // MOD_BLOCK,),
        in_specs=[
            pl.BlockSpec((batch, d), lambda j: (0, 0)),
            pl.BlockSpec((d, MOD_BLOCK), lambda j: (0, j)),
            pl.BlockSpec((1, MOD_BLOCK), lambda j: (0, j)),
        ],
        out_specs=pl.BlockSpec((batch, MOD_BLOCK), lambda j: (0, j)),
        out_shape=jax.ShapeDtypeStruct((batch, n), F32),
        compiler_params=pltpu.CompilerParams(dimension_semantics=("arbitrary",)),
    )(c, w, b.reshape(1, n))


def _ffn_kernel(x_ref, mod_ref, nw_ref, w13_hbm, w2_hbm, *rest, mod_base, final_norm):
    o_ref, w13_ref, w2_ref, s_ref, stage13_ref, stage2_ref, sem_ref = rest[-7:]

    @pl.when(_is_first_step())
    def _():
        _load_weight_bf16(w13_hbm, w13_ref, stage13_ref, sem_ref)
        _load_weight_bf16(w2_hbm, w2_ref, stage2_ref, sem_ref)

    x = x_ref[...]
    shift, scale, gate = _mod_rows(mod_ref, mod_base)
    hb = _rms_mod(x, nw_ref[...], shift, scale).astype(BF16)

    def up(lhs, lo):
        return (_dot(lhs, w13_ref[:, lo:lo + FFN_BLOCK]),
                _dot(lhs, w13_ref[:, D_FF + lo:D_FF + lo + FFN_BLOCK]))

    for lo in range(0, D_FF, FFN_BLOCK):
        if lo == 0:
            parts = [up(hb[r:r + HEAD_ROWS], lo) for r in range(0, x.shape[0], HEAD_ROWS)]
            a = jnp.concatenate([p[0] for p in parts], axis=0)
            b = jnp.concatenate([p[1] for p in parts], axis=0)
        else:
            a, b = up(hb, lo)
        s_ref[:, lo:lo + FFN_BLOCK] = (_silu(a) * b).astype(BF16)
    for r in range(0, x.shape[0], TAIL_ROWS):
        rows = slice(r, r + TAIL_ROWS)
        out = x[rows] + gate * (0.5 * _dot(s_ref[rows, :], w2_ref[...]))
        if final_norm:
            ms = jnp.mean(out * out, axis=-1, keepdims=True)
            out = out * lax.rsqrt(ms + EPS) * rest[0][...]
        o_ref[rows, :] = out


def _ffn(x, mod, mod_base, norm_w, w13, w2, final_w):
    batch, seq, d = x.shape
    tm = FFN_TOKENS
    final_norm = final_w is not None
    in_specs = [
        pl.BlockSpec((None, tm, d), lambda b, i: (b, i, 0)),
        _resident(mod.shape),
        _resident((1, d)),
        pl.BlockSpec(memory_space=pl.ANY),
        pl.BlockSpec(memory_space=pl.ANY),
    ]
    args = [x, mod, norm_w.reshape(1, d), w13, w2]
    if final_norm:
        in_specs.append(_resident((1, d)))
        args.append(final_w.reshape(1, d))
    return pl.pallas_call(
        functools.partial(_ffn_kernel, mod_base=mod_base, final_norm=final_norm),
        grid=(batch, seq // tm),
        in_specs=in_specs,
        out_specs=pl.BlockSpec((None, tm, d), lambda b, i: (b, i, 0)),
        out_shape=jax.ShapeDtypeStruct(x.shape, F32),
        scratch_shapes=[
            pltpu.VMEM(w13.shape, BF16),
            pltpu.VMEM(w2.shape, BF16),
            pltpu.VMEM((tm, D_FF), BF16),
            pltpu.VMEM((STAGE_SLOTS, FFN_STAGE_ROWS[0], w13.shape[1]), F32),
            pltpu.VMEM((STAGE_SLOTS, FFN_STAGE_ROWS[1], w2.shape[1]), F32),
            pltpu.SemaphoreType.DMA((STAGE_SLOTS,)),
        ],
        compiler_params=pltpu.CompilerParams(
            dimension_semantics=("arbitrary", "arbitrary"),
            vmem_limit_bytes=V7X_VMEM_LIMIT_BYTES),
    )(*args)


def _rope(t, cos, sin):
    t1, t2 = t[:, :ROPE_HALF], t[:, ROPE_HALF:]
    return jnp.concatenate([t1 * cos - t2 * sin, t1 * sin + t2 * cos], axis=-1)


def _mixer_kernel(x_ref, mod_ref, nw_ref, cos_ref, sin_ref, idec_ref, qdec_ref, kdec_ref,
                  cdec_ref, w_in_hbm, gnw_ref, w_r_hbm, plin_hbm, pscale_ref, w_p_hbm,
                  w_out_hbm, o_ref, state_ref, uext_ref, gated_ref,
                  w_in_ref, w_r_ref, plin_ref, w_p_ref, w_out_ref,
                  stage_in_ref, stage_ref, stage_plin_ref, sem_ref):
    i = pl.program_id(1)
    tm = x_ref.shape[0]
    n_chunks = tm // RET_CHUNK

    @pl.when(_is_first_step())
    def _():
        _load_weight_bf16(w_in_hbm, w_in_ref, stage_in_ref, sem_ref)
        _load_weight_bf16(w_r_hbm, w_r_ref, stage_ref, sem_ref)
        _load_weight_bf16(w_p_hbm, w_p_ref, stage_ref, sem_ref)
        _load_weight_bf16(w_out_hbm, w_out_ref, stage_ref, sem_ref)
        _load_weight_bf16(plin_hbm, plin_ref, stage_plin_ref, sem_ref)
        for gi in range(len(POOL_WINDOWS)):
            cols = slice(gi * POOL_GROUP_W, (gi + 1) * POOL_GROUP_W)
            lin = plin_ref[cols, :].astype(F32) * pscale_ref[:, cols]
            hi = lin.astype(BF16)
            lo = (lin - hi.astype(F32)).astype(BF16)
            w_g = w_p_ref[cols, :]
            w_p_ref[cols, :] = (_dot(hi, w_g) + _dot(lo, w_g)).astype(BF16)

    @pl.when(i == 0)
    def _():
        state_ref[...] = jnp.zeros(state_ref.shape, F32)
        uext_ref[0:POOL_HALO, :] = jnp.zeros((POOL_HALO, POOL_W), F32)

    x = x_ref[...]
    shift, scale, gate = _mod_rows(mod_ref, 3)
    hb = _rms_mod(x, nw_ref[...], shift, scale).astype(BF16)
    cos = cos_ref[...]
    sin = sin_ref[...]

    def proj(off, width):
        return _dot(hb, w_in_ref[:, off:off + width])

    qk, roped, vg, scores, cross = {}, {}, {}, {}, {}

    def proj_qk(h):
        qk[h] = (proj(OFF_Q + h * RET_DK, RET_DK), proj(OFF_K + h * RET_DK, RET_DK))

    def chunk_rows(c):
        return slice(c * RET_CHUNK, (c + 1) * RET_CHUNK)

    def rope_qk(h):
        q, k = qk.pop(h)
        qr = _rope(q, cos, sin)
        kr = _rope(k, cos, sin) * (RET_DK ** -0.5)
        qdec, kdec = qdec_ref[h], kdec_ref[h]
        roped[h] = [(qr[chunk_rows(c)].astype(BF16), (qr[chunk_rows(c)] * qdec).astype(BF16),
                     kr[chunk_rows(c)].astype(BF16), (kr[chunk_rows(c)] * kdec).astype(BF16))
                    for c in range(n_chunks)]

    def proj_vg(h):
        v = proj(OFF_V + h * RET_DV, RET_DV)
        vg[h] = (v.astype(BF16), proj(OFF_G + h * RET_DV, RET_DV))

    def state_step(h, c):
        _, qd, _, kd = roped[h][c]
        st = state_ref[h]
        cross[h].append(_dot(qd, st.astype(BF16)))
        upd = lax.dot_general(kd, vg[h][0][chunk_rows(c)], (((0,), (0,)), ((), ())),
                              preferred_element_type=F32)
        state_ref[h] = cdec_ref[h] * st + upd

    def retention_first(h):
        scores[h] = [lax.dot_general(qb, kb, (((1,), (1,)), ((), ())),
                                     preferred_element_type=F32)
                     for qb, _, kb, _ in roped[h]]
        cross[h] = []
        state_step(h, 0)

    def retention_rest(h):
        for c in range(1, n_chunks):
            state_step(h, c)
        roped.pop(h)

    def finish_head(h):
        vb, g = vg.pop(h)
        idec = idec_ref[h]
        outs = [_dot((s * idec).astype(BF16), vb[chunk_rows(c)]) + cr
                for c, (s, cr) in enumerate(zip(scores.pop(h), cross.pop(h)))]
        o = outs[0] if n_chunks == 1 else jnp.concatenate(outs, axis=0)
        mu = jnp.mean(o, axis=-1, keepdims=True)
        oc = o - mu
        var = jnp.mean(oc * oc, axis=-1, keepdims=True)
        ret = oc * lax.rsqrt(var + EPS) * gnw_ref[:, h * RET_DV:(h + 1) * RET_DV]
        gated_ref[:, h * RET_DV:(h + 1) * RET_DV] = (_silu(g) * ret).astype(BF16)

    def ret_branch(h):
        return _dot(gated_ref[:, h * RET_DV:(h + 1) * RET_DV],
                    w_r_ref[h * RET_DV:(h + 1) * RET_DV, :])

    def pool_windows(u):
        uext_ref[POOL_HALO:POOL_HALO + tm, :] = u
        pos = (i * tm + lax.broadcasted_iota(jnp.int32, (tm, 1), 0)).astype(F32)
        pooled = []
        for gi, w in enumerate(POOL_WINDOWS):
            cols = slice(gi * POOL_GROUP_W, (gi + 1) * POOL_GROUP_W)
            s = uext_ref[:, cols]
            span = 1
            while span < w:
                s = s + pltpu.roll(s, span, axis=0)
                span *= 2
            count = jnp.minimum(pos + 1.0, float(w))
            pooled.append((s[POOL_HALO:, :] / count - u[:, cols]).astype(BF16))
        uext_ref[0:POOL_HALO, :] = uext_ref[tm:tm + POOL_HALO, :]
        return jnp.concatenate(pooled, axis=1)

    proj_qk(0)
    u = proj(OFF_U, POOL_W)
    proj_vg(0)
    rope_qk(0)
    pooled = pool_windows(u)
    y_r = None
    for h in range(RET_HEADS):
        last = h + 1 == RET_HEADS
        if not last:
            proj_qk(h + 1)
        retention_first(h)
        if not last:
            rope_qk(h + 1)
            proj_vg(h + 1)
        else:
            a_r = proj(OFF_AR, D_MODEL)
            a_p = proj(OFF_AP, D_MODEL)
        retention_rest(h)
        finish_head(h)
        if h == 0:
            y_p = _dot(pooled, w_p_ref[...])
        if h >= 1:
            part = ret_branch(h - 1)
            y_r = part if y_r is None else y_r + part
    y_r = y_r + ret_branch(RET_HEADS - 1)

    half = tm // 2
    for r in range(2):
        rows = slice(r * half, (r + 1) * half)
        merged = (jax.nn.sigmoid(a_r[rows]) * y_r[rows]
                  + jax.nn.sigmoid(a_p[rows]) * y_p[rows])
        o_ref[rows, :] = x[rows] + gate * _dot(merged.astype(BF16), w_out_ref[...])


def _retention_tables():
    c = RET_CHUNK
    log_gamma = np.log1p(-(2.0 ** (-5.0 - np.arange(RET_HEADS, dtype=np.float32)))).astype(np.float32)
    idx = np.arange(c, dtype=np.float32)
    diff = idx[:, None] - idx[None, :]
    inner = np.where(diff >= 0, np.exp(log_gamma[:, None, None] * np.maximum(diff, 0.0)), 0.0)
    q_decay = np.exp(log_gamma[:, None] * (idx + 1.0))
    k_decay = np.exp(log_gamma[:, None] * (c - 1.0 - idx))
    chunk_decay = np.exp(log_gamma * c)
    wide = (RET_HEADS, c, RET_DK)
    return (inner.astype(np.float32),
            np.ascontiguousarray(np.broadcast_to(q_decay[:, :, None], wide), dtype=np.float32),
            np.ascontiguousarray(np.broadcast_to(k_decay[:, :, None], wide), dtype=np.float32),
            chunk_decay.astype(np.float32))


def _rope_tables(seq):
    inv = (1.0 / (ROPE_BASE ** (np.arange(ROPE_HALF, dtype=np.float32) / ROPE_HALF))).astype(np.float32)
    ang = np.arange(seq, dtype=np.float32)[:, None] * inv[None, :]
    return np.cos(ang).astype(np.float32), np.sin(ang).astype(np.float32)


def _mixer(x, mod, norm_w, w_in, gn_w, w_r, pool_lin, pool_scale, w_p, w_out):
    batch, seq, d = x.shape
    tm = MIX_TOKENS
    cos, sin = _rope_tables(seq)
    idec, qdec, kdec, cdec = _retention_tables()
    in_specs = [
        pl.BlockSpec((None, tm, d), lambda b, i: (b, i, 0)),
        _resident(mod.shape),
        _resident((1, d)),
        pl.BlockSpec((tm, ROPE_HALF), lambda b, i: (i, 0)),
        pl.BlockSpec((tm, ROPE_HALF), lambda b, i: (i, 0)),
        _resident(idec.shape),
        _resident(qdec.shape),
        _resident(kdec.shape),
        pl.BlockSpec(memory_space=pltpu.SMEM),
        pl.BlockSpec(memory_space=pl.ANY),
        _resident((1, RET_HEADS * RET_DV)),
        pl.BlockSpec(memory_space=pl.ANY),
        pl.BlockSpec(memory_space=pl.ANY),
        _resident((1, POOL_W)),
        pl.BlockSpec(memory_space=pl.ANY),
        pl.BlockSpec(memory_space=pl.ANY),
    ]
    pool_lin = pool_lin.reshape(POOL_W, POOL_GROUP_W)
    return pl.pallas_call(
        _mixer_kernel,
        grid=(batch, seq // tm),
        in_specs=in_specs,
        out_specs=pl.BlockSpec((None, tm, d), lambda b, i: (b, i, 0)),
        out_shape=jax.ShapeDtypeStruct(x.shape, F32),
        scratch_shapes=[
            pltpu.VMEM((RET_HEADS, RET_DK, RET_DV), F32),
            pltpu.VMEM((POOL_HALO + tm, POOL_W), F32),
            pltpu.VMEM((tm, RET_HEADS * RET_DV), BF16),
            pltpu.VMEM(w_in.shape, BF16),
            pltpu.VMEM(w_r.shape, BF16),
            pltpu.VMEM(pool_lin.shape, BF16),
            pltpu.VMEM(w_p.shape, BF16),
            pltpu.VMEM(w_out.shape, BF16),
            pltpu.VMEM((STAGE_SLOTS, MIX_STAGE_ROWS[0], w_in.shape[1]), F32),
            pltpu.VMEM((STAGE_SLOTS, MIX_STAGE_ROWS[1], d), F32),
            pltpu.VMEM((STAGE_SLOTS, MIX_STAGE_ROWS[1], POOL_GROUP_W), F32),
            pltpu.SemaphoreType.DMA((STAGE_SLOTS,)),
        ],
        compiler_params=pltpu.CompilerParams(
            dimension_semantics=("arbitrary", "arbitrary"),
            vmem_limit_bytes=V7X_VMEM_LIMIT_BYTES),
    )(x, mod, norm_w.reshape(1, d), cos, sin, idec, qdec, kdec, cdec, w_in,
      gn_w.reshape(1, -1), w_r, pool_lin, pool_scale.reshape(1, -1), w_p, w_out)


def kernel(x, c, ada_w, ada_b, norm_ffn1, ffn1_w13, ffn1_w2, norm_mix, w_in, ret_gn_w,
           w_ret_branch, pool_lin, pool_scale, w_pool_branch, w_out, norm_ffn2, ffn2_w13,
           ffn2_w2, norm_final):
    depth = ada_w.shape[0]
    for l in range(depth):
        mod = _modulation(c, ada_w[l], ada_b[l])
        x = _ffn(x, mod, 0, norm_ffn1[l], ffn1_w13[l], ffn1_w2[l], None)
        x = _mixer(x, mod, norm_mix[l], w_in[l], ret_gn_w[l], w_ret_branch[l], pool_lin[l],
                   pool_scale[l], w_pool_branch[l], w_out[l])
        x = _ffn(x, mod, 6, norm_ffn2[l], ffn2_w13[l], ffn2_w2[l],
                 norm_final if l == depth - 1 else None)
    return x
```

```python
import functools

import jax
import jax.numpy as jnp
import numpy as np
from jax import lax
from jax.experimental import pallas as pl
from jax.experimental.pallas import tpu as pltpu

F32 = jnp.float32
BF16 = jnp.bfloat16

D_MODEL = 1024
N_MOD = 9
EPS = 1e-6
D_FF = 2816

RET_HEADS = 4
RET_DK = 256
RET_DV = 512
RET_CHUNK = 256
ROPE_BASE = 10000.0
ROPE_HALF = RET_DK // 2

POOL_WINDOWS = (2, 4, 8, 16)
POOL_GROUP_W = 256
POOL_W = len(POOL_WINDOWS) * POOL_GROUP_W
POOL_HALO = 16

OFF_Q = 0
OFF_K = OFF_Q + RET_HEADS * RET_DK
OFF_V = OFF_K + RET_HEADS * RET_DK
OFF_G = OFF_V + RET_HEADS * RET_DV
OFF_U = OFF_G + RET_HEADS * RET_DV
OFF_AR = OFF_U + POOL_W
OFF_AP = OFF_AR + D_MODEL
IN_W = OFF_AP + D_MODEL

V7X_VMEM_LIMIT_BYTES = 61 * 1024 * 1024
V7X_MXU_WIDTH = 256

FFN_TOKENS = 1024
HEAD_ROWS = 128
TAIL_ROWS = 256
FFN_BLOCK = V7X_MXU_WIDTH
MIX_TOKENS = 2 * RET_CHUNK
MOD_BLOCK = 1024
STAGE_SLOTS = 4
FFN_STAGE_ROWS = (64, 256)
MIX_STAGE_ROWS = (16, 64)


def _dot(a, b):
    return jnp.dot(a, b, preferred_element_type=F32)


def _silu(a):
    return a * jax.nn.sigmoid(a)


def _rms_mod(x, norm_w, shift, scale):
    ms = jnp.mean(x * x, axis=-1, keepdims=True)
    y = x * lax.rsqrt(ms + EPS) * norm_w
    return y * (1.0 + scale) + shift


def _resident(shape):
    zeros = (0,) * len(shape)
    return pl.BlockSpec(shape, lambda *_: zeros, pipeline_mode=pl.Buffered(1))


def _is_first_step():
    return jnp.logical_and(pl.program_id(0) == 0, pl.program_id(1) == 0)


def _mod_rows(mod_ref, first):
    b = pl.program_id(0)
    return [mod_ref[pl.ds(b, 1), (first + k) * D_MODEL:(first + k + 1) * D_MODEL]
            for k in range(3)]


def _load_weight_bf16(src_ref, dst_ref, stage_ref, sem_ref):
    slots, rows = stage_ref.shape[:2]
    n = src_ref.shape[0] // rows

    def copy(c):
        return pltpu.make_async_copy(src_ref.at[pl.ds(c * rows, rows)], stage_ref.at[c % slots],
                                     sem_ref.at[c % slots])

    for c in range(min(slots, n)):
        copy(c).start()
    for c in range(n):
        copy(c).wait()
        dst_ref[c * rows:(c + 1) * rows, :] = stage_ref[c % slots].astype(BF16)
        if c + slots < n:
            copy(c + slots).start()


def _mod_kernel(c_ref, w_ref, b_ref, o_ref):
    c = c_ref[...]
    o_ref[...] = _dot(_silu(c).astype(BF16), w_ref[...].astype(BF16)) + b_ref[...]


def _modulation(c, w, b):
    batch, d = c.shape
    n = w.shape[1]
    return pl.pallas_call(
        _mod_kernel,
        grid=(n // MOD_BLOCK,),
        in_specs=[
            pl.BlockSpec((batch, d), lambda j: (0, 0)),
            pl.BlockSpec((d, MOD_BLOCK), lambda j: (0, j)),
            pl.BlockSpec((1, MOD_BLOCK), lambda j: (0, j)),
        ],
        out_specs=pl.BlockSpec((batch, MOD_BLOCK), lambda j: (0, j)),
        out_shape=jax.ShapeDtypeStruct((batch, n), F32),
        compiler_params=pltpu.CompilerParams(dimension_semantics=("arbitrary",)),
    )(c, w, b.reshape(1, n))


def _ffn_kernel(x_ref, mod_ref, nw_ref, w13_hbm, w2_hbm, *rest, mod_base, final_norm):
    o_ref, w13_ref, w2_ref, s_ref, stage13_ref, stage2_ref, sem_ref = rest[-7:]

    @pl.when(_is_first_step())
    def _():
        _load_weight_bf16(w13_hbm, w13_ref, stage13_ref, sem_ref)
        _load_weight_bf16(w2_hbm, w2_ref, stage2_ref, sem_ref)

    x = x_ref[...]
    shift, scale, gate = _mod_rows(mod_ref, mod_base)
    hb = _rms_mod(x, nw_ref[...], shift, scale).astype(BF16)

    def up(lhs, lo):
        return (_dot(lhs, w13_ref[:, lo:lo + FFN_BLOCK]),
                _dot(lhs, w13_ref[:, D_FF + lo:D_FF + lo + FFN_BLOCK]))

    for lo in range(0, D_FF, FFN_BLOCK):
        if lo == 0:
            parts = [up(hb[r:r + HEAD_ROWS], lo) for r in range(0, x.shape[0], HEAD_ROWS)]
            a = jnp.concatenate([p[0] for p in parts], axis=0)
            b = jnp.concatenate([p[1] for p in parts], axis=0)
        else:
            a, b = up(hb, lo)
        s_ref[:, lo:lo + FFN_BLOCK] = (_silu(a) * b).astype(BF16)
    for r in range(0, x.shape[0], TAIL_ROWS):
        rows = slice(r, r + TAIL_ROWS)
        out = x[rows] + gate * (0.5 * _dot(s_ref[rows, :], w2_ref[...]))
        if final_norm:
            ms = jnp.mean(out * out, axis=-1, keepdims=True)
            out = out * lax.rsqrt(ms + EPS) * rest[0][...]
        o_ref[rows, :] = out


def _ffn(x, mod, mod_base, norm_w, w13, w2, final_w):
    batch, seq, d = x.shape
    tm = FFN_TOKENS
    final_norm = final_w is not None
    in_specs = [
        pl.BlockSpec((None, tm, d), lambda b, i: (b, i, 0)),
        _resident(mod.shape),
        _resident((1, d)),
        pl.BlockSpec(memory_space=pl.ANY),
        pl.BlockSpec(memory_space=pl.ANY),
    ]
    args = [x, mod, norm_w.reshape(1, d), w13, w2]
    if final_norm:
        in_specs.append(_resident((1, d)))
        args.append(final_w.reshape(1, d))
    return pl.pallas_call(
        functools.partial(_ffn_kernel, mod_base=mod_base, final_norm=final_norm),
        grid=(batch, seq // tm),
        in_specs=in_specs,
        out_specs=pl.BlockSpec((None, tm, d), lambda b, i: (b, i, 0)),
        out_shape=jax.ShapeDtypeStruct(x.shape, F32),
        scratch_shapes=[
            pltpu.VMEM(w13.shape, BF16),
            pltpu.VMEM(w2.shape, BF16),
            pltpu.VMEM((tm, D_FF), BF16),
            pltpu.VMEM((STAGE_SLOTS, FFN_STAGE_ROWS[0], w13.shape[1]), F32),
            pltpu.VMEM((STAGE_SLOTS, FFN_STAGE_ROWS[1], w2.shape[1]), F32),
            pltpu.SemaphoreType.DMA((STAGE_SLOTS,)),
        ],
        compiler_params=pltpu.CompilerParams(
            dimension_semantics=("arbitrary", "arbitrary"),
            vmem_limit_bytes=V7X_VMEM_LIMIT_BYTES),
    )(*args)


def _rope(t, cos, sin):
    t1, t2 = t[:, :ROPE_HALF], t[:, ROPE_HALF:]
    return jnp.concatenate([t1 * cos - t2 * sin, t1 * sin + t2 * cos], axis=-1)


def _mixer_kernel(x_ref, mod_ref, nw_ref, cos_ref, sin_ref, idec_ref, qdec_ref, kdec_ref,
                  cdec_ref, w_in_hbm, gnw_ref, w_r_hbm, plin_hbm, pscale_ref, w_p_hbm,
                  w_out_hbm, o_ref, state_ref, uext_ref, gated_ref,
                  w_in_ref, w_r_ref, plin_ref, w_p_ref, w_out_ref,
                  stage_in_ref, stage_ref, stage_plin_ref, sem_ref):
    i = pl.program_id(1)
    tm = x_ref.shape[0]
    n_chunks = tm // RET_CHUNK

    @pl.when(_is_first_step())
    def _():
        _load_weight_bf16(w_in_hbm, w_in_ref, stage_in_ref, sem_ref)
        _load_weight_bf16(w_r_hbm, w_r_ref, stage_ref, sem_ref)
        _load_weight_bf16(w_p_hbm, w_p_ref, stage_ref, sem_ref)
        _load_weight_bf16(w_out_hbm, w_out_ref, stage_ref, sem_ref)
        _load_weight_bf16(plin_hbm, plin_ref, stage_plin_ref, sem_ref)
        for gi in range(len(POOL_WINDOWS)):
            cols = slice(gi * POOL_GROUP_W, (gi + 1) * POOL_GROUP_W)
            lin = plin_ref[cols, :].astype(F32) * pscale_ref[:, cols]
            hi = lin.astype(BF16)
            lo = (lin - hi.astype(F32)).astype(BF16)
            w_g = w_p_ref[cols, :]
            w_p_ref[cols, :] = (_dot(hi, w_g) + _dot(lo, w_g)).astype(BF16)

    @pl.when(i == 0)
    def _():
        state_ref[...] = jnp.zeros(state_ref.shape, F32)
        uext_ref[0:POOL_HALO, :] = jnp.zeros((POOL_HALO, POOL_W), F32)

    x = x_ref[...]
    shift, scale, gate = _mod_rows(mod_ref, 3)
    hb = _rms_mod(x, nw_ref[...], shift, scale).astype(BF16)
    cos = cos_ref[...]
    sin = sin_ref[...]

    def proj(off, width):
        return _dot(hb, w_in_ref[:, off:off + width])

    qk, roped, vg, scores, start_state = {}, {}, {}, {}, {}

    def proj_qk(h):
        qk[h] = (proj(OFF_Q + h * RET_DK, RET_DK), proj(OFF_K + h * RET_DK, RET_DK))

    def chunk_rows(c):
        return slice(c * RET_CHUNK, (c + 1) * RET_CHUNK)

    def rope_qk(h):
        q, k = qk.pop(h)
        qr = _rope(q, cos, sin)
        kr = _rope(k, cos, sin) * (RET_DK ** -0.5)
        qdec = jnp.concatenate([qdec_ref[h]] * (RET_DK // ROPE_HALF), axis=1)
        kdec = jnp.concatenate([kdec_ref[h]] * (RET_DK // ROPE_HALF), axis=1)
        roped[h] = [(qr[chunk_rows(c)].astype(BF16), (qr[chunk_rows(c)] * qdec).astype(BF16),
                     kr[chunk_rows(c)].astype(BF16), (kr[chunk_rows(c)] * kdec).astype(BF16))
                    for c in range(n_chunks)]

    def proj_vg(h):
        v = proj(OFF_V + h * RET_DV, RET_DV)
        vg[h] = (v.astype(BF16), proj(OFF_G + h * RET_DV, RET_DV))

    def state_step(h, c):
        _, _, _, kd = roped[h][c]
        st = state_ref[h]
        start_state[h].append(st.astype(BF16))
        upd = lax.dot_general(kd, vg[h][0][chunk_rows(c)], (((0,), (0,)), ((), ())),
                              preferred_element_type=F32)
        state_ref[h] = cdec_ref[h] * st + upd

    def retention_first(h):
        scores[h] = [lax.dot_general(qb, kb, (((1,), (1,)), ((), ())),
                                     preferred_element_type=F32)
                     for qb, _, kb, _ in roped[h]]
        start_state[h] = []
        state_step(h, 0)

    def retention_rest(h):
        for c in range(1, n_chunks):
            state_step(h, c)

    def finish_head(h):
        vb, g = vg.pop(h)
        idec = idec_ref[h]
        outs = [_dot(jnp.concatenate([(s * idec).astype(BF16), qd], axis=1),
                     jnp.concatenate([vb[chunk_rows(c)], stb], axis=0))
                for c, (s, (_, qd, _, _), stb) in enumerate(
                    zip(scores.pop(h), roped.pop(h), start_state.pop(h)))]
        o = outs[0] if n_chunks == 1 else jnp.concatenate(outs, axis=0)
        mu = jnp.mean(o, axis=-1, keepdims=True)
        oc = o - mu
        var = jnp.mean(oc * oc, axis=-1, keepdims=True)
        ret = oc * lax.rsqrt(var + EPS) * gnw_ref[:, h * RET_DV:(h + 1) * RET_DV]
        gated_ref[:, h * RET_DV:(h + 1) * RET_DV] = (_silu(g) * ret).astype(BF16)

    def pool_windows(u):
        uext_ref[POOL_HALO:POOL_HALO + tm, :] = u
        pos = (i * tm + lax.broadcasted_iota(jnp.int32, (tm, 1), 0)).astype(F32)
        pooled = []
        for gi, w in enumerate(POOL_WINDOWS):
            cols = slice(gi * POOL_GROUP_W, (gi + 1) * POOL_GROUP_W)
            s = uext_ref[:, cols]
            span = 1
            while span < w:
                s = s + pltpu.roll(s, span, axis=0)
                span *= 2
            count = jnp.minimum(pos + 1.0, float(w))
            pooled.append((s[POOL_HALO:, :] / count - u[:, cols]).astype(BF16))
        uext_ref[0:POOL_HALO, :] = uext_ref[tm:tm + POOL_HALO, :]
        return jnp.concatenate(pooled, axis=1)

    proj_qk(0)
    u = proj(OFF_U, POOL_W)
    proj_vg(0)
    rope_qk(0)
    pooled = pool_windows(u)
    for h in range(RET_HEADS):
        last = h + 1 == RET_HEADS
        if not last:
            proj_qk(h + 1)
        retention_first(h)
        if not last:
            rope_qk(h + 1)
            proj_vg(h + 1)
        else:
            a_r = proj(OFF_AR, D_MODEL)
            a_p = proj(OFF_AP, D_MODEL)
        retention_rest(h)
        finish_head(h)
    y_p = _dot(pooled, w_p_ref[...])

    chunks = [chunk_rows(c) for c in range(n_chunks)]
    y_r = [_dot(gated_ref[rows, :], w_r_ref[...]) for rows in chunks[:1]]
    for c, rows in enumerate(chunks):
        if c + 1 < n_chunks:
            y_r.append(_dot(gated_ref[chunks[c + 1], :], w_r_ref[...]))
        merged = (jax.nn.sigmoid(a_r[rows]) * y_r[c]
                  + jax.nn.sigmoid(a_p[rows]) * y_p[rows])
        o_ref[rows, :] = x[rows] + gate * _dot(merged.astype(BF16), w_out_ref[...])


def _retention_tables():
    c = RET_CHUNK
    log_gamma = np.log1p(-(2.0 ** (-5.0 - np.arange(RET_HEADS, dtype=np.float32)))).astype(np.float32)
    idx = np.arange(c, dtype=np.float32)
    diff = idx[:, None] - idx[None, :]
    inner = np.where(diff >= 0, np.exp(log_gamma[:, None, None] * np.maximum(diff, 0.0)), 0.0)
    q_decay = np.exp(log_gamma[:, None] * (idx + 1.0))
    k_decay = np.exp(log_gamma[:, None] * (c - 1.0 - idx))
    chunk_decay = np.exp(log_gamma * c)
    wide = (RET_HEADS, c, ROPE_HALF)
    return (inner.astype(np.float32),
            np.ascontiguousarray(np.broadcast_to(q_decay[:, :, None], wide), dtype=np.float32),
            np.ascontiguousarray(np.broadcast_to(k_decay[:, :, None], wide), dtype=np.float32),
            chunk_decay.astype(np.float32))


def _rope_tables(seq):
    inv = (1.0 / (ROPE_BASE ** (np.arange(ROPE_HALF, dtype=np.float32) / ROPE_HALF))).astype(np.float32)
    ang = np.arange(seq, dtype=np.float32)[:, None] * inv[None, :]
    return np.cos(ang).astype(np.float32), np.sin(ang).astype(np.float32)


def _mixer(x, mod, norm_w, w_in, gn_w, w_r, pool_lin, pool_scale, w_p, w_out):
    batch, seq, d = x.shape
    tm = MIX_TOKENS
    cos, sin = _rope_tables(seq)
    idec, qdec, kdec, cdec = _retention_tables()
    in_specs = [
        pl.BlockSpec((None, tm, d), lambda b, i: (b, i, 0)),
        _resident(mod.shape),
        _resident((1, d)),
        pl.BlockSpec((tm, ROPE_HALF), lambda b, i: (i, 0)),
        pl.BlockSpec((tm, ROPE_HALF), lambda b, i: (i, 0)),
        _resident(idec.shape),
        _resident(qdec.shape),
        _resident(kdec.shape),
        pl.BlockSpec(memory_space=pltpu.SMEM),
        pl.BlockSpec(memory_space=pl.ANY),
        _resident((1, RET_HEADS * RET_DV)),
        pl.BlockSpec(memory_space=pl.ANY),
        pl.BlockSpec(memory_space=pl.ANY),
        _resident((1, POOL_W)),
        pl.BlockSpec(memory_space=pl.ANY),
        pl.BlockSpec(memory_space=pl.ANY),
    ]
    pool_lin = pool_lin.reshape(POOL_W, POOL_GROUP_W)
    return pl.pallas_call(
        _mixer_kernel,
        grid=(batch, seq // tm),
        in_specs=in_specs,
        out_specs=pl.BlockSpec((None, tm, d), lambda b, i: (b, i, 0)),
        out_shape=jax.ShapeDtypeStruct(x.shape, F32),
        scratch_shapes=[
            pltpu.VMEM((RET_HEADS, RET_DK, RET_DV), F32),
            pltpu.VMEM((POOL_HALO + tm, POOL_W), F32),
            pltpu.VMEM((tm, RET_HEADS * RET_DV), BF16),
            pltpu.VMEM(w_in.shape, BF16),
            pltpu.VMEM(w_r.shape, BF16),
            pltpu.VMEM(pool_lin.shape, BF16),
            pltpu.VMEM(w_p.shape, BF16),
            pltpu.VMEM(w_out.shape, BF16),
            pltpu.VMEM((STAGE_SLOTS, MIX_STAGE_ROWS[0], w_in.shape[1]), F32),
            pltpu.VMEM((STAGE_SLOTS, MIX_STAGE_ROWS[1], d), F32),
            pltpu.VMEM((STAGE_SLOTS, MIX_STAGE_ROWS[1], POOL_GROUP_W), F32),
            pltpu.SemaphoreType.DMA((STAGE_SLOTS,)),
        ],
        compiler_params=pltpu.CompilerParams(
            dimension_semantics=("arbitrary", "arbitrary"),
            vmem_limit_bytes=V7X_VMEM_LIMIT_BYTES),
    )(x, mod, norm_w.reshape(1, d), cos, sin, idec, qdec, kdec, cdec, w_in,
      gn_w.reshape(1, -1), w_r, pool_lin, pool_scale.reshape(1, -1), w_p, w_out)


def kernel(x, c, ada_w, ada_b, norm_ffn1, ffn1_w13, ffn1_w2, norm_mix, w_in, ret_gn_w,
           w_ret_branch, pool_lin, pool_scale, w_pool_branch, w_out, norm_ffn2, ffn2_w13,
           ffn2_w2, norm_final):
    depth = ada_w.shape[0]
    for l in range(depth):
        mod = _modulation(c, ada_w[l], ada_b[l])
        x = _ffn(x, mod, 0, norm_ffn1[l], ffn1_w13[l], ffn1_w2[l], None)
        x = _mixer(x, mod, norm_mix[l], w_in[l], ret_gn_w[l], w_ret_branch[l], pool_lin[l],
                   pool_scale[l], w_pool_branch[l], w_out[l])
        x = _ffn(x, mod, 6, norm_ffn2[l], ffn2_w13[l], ffn2_w2[l],
                 norm_final if l == depth - 1 else None)
    return x
```

```python
import functools

import jax
import jax.numpy as jnp
import numpy as np
from jax import lax
from jax.experimental import pallas as pl
from jax.experimental.pallas import tpu as pltpu

F32 = jnp.float32
BF16 = jnp.bfloat16

D_MODEL = 1024
N_MOD = 9
EPS = 1e-6
D_FF = 2816

RET_HEADS = 4
RET_DK = 256
RET_DV = 512
RET_CHUNK = 256
ROPE_BASE = 10000.0
ROPE_HALF = RET_DK // 2

POOL_WINDOWS = (2, 4, 8, 16)
POOL_GROUP_W = 256
POOL_W = len(POOL_WINDOWS) * POOL_GROUP_W
POOL_HALO = 16

OFF_Q = 0
OFF_K = OFF_Q + RET_HEADS * RET_DK
OFF_V = OFF_K + RET_HEADS * RET_DK
OFF_G = OFF_V + RET_HEADS * RET_DV
OFF_U = OFF_G + RET_HEADS * RET_DV
OFF_AR = OFF_U + POOL_W
OFF_AP = OFF_AR + D_MODEL
IN_W = OFF_AP + D_MODEL

V7X_VMEM_LIMIT_BYTES = 61 * 1024 * 1024
V7X_MXU_WIDTH = 256

FFN_TOKENS = 1024
HEAD_ROWS = 128
TAIL_ROWS = 256
FFN_BLOCK = V7X_MXU_WIDTH
MIX_TOKENS = 2 * RET_CHUNK
MOD_BLOCK = 2304
STAGE_SLOTS = 4
FFN_STAGE_ROWS = (64, 256)
MIX_STAGE_SLOTS = 6
MIX_STAGE_ROWS = 128
POOL_STAGE_ROWS = 64


def _dot(a, b):
    return jnp.dot(a, b, preferred_element_type=F32)


def _silu(a):
    return a * jax.nn.sigmoid(a)


def _rms_mod(x, norm_w, shift, scale):
    ms = jnp.mean(x * x, axis=-1, keepdims=True)
    y = x * lax.rsqrt(ms + EPS) * norm_w
    return y * (1.0 + scale) + shift


def _resident(shape):
    zeros = (0,) * len(shape)
    return pl.BlockSpec(shape, lambda *_: zeros, pipeline_mode=pl.Buffered(1))


def _is_first_step():
    return jnp.logical_and(pl.program_id(0) == 0, pl.program_id(1) == 0)


def _mod_rows(mod_ref, first):
    b = pl.program_id(0)
    return [mod_ref[pl.ds(b, 1), (first + k) * D_MODEL:(first + k + 1) * D_MODEL]
            for k in range(3)]


def _load_weight_bf16(src_ref, dst_ref, stage_ref, sem_ref):
    slots, rows, cols = stage_ref.shape
    tiles = [(r, c) for c in range(0, src_ref.shape[1], cols)
             for r in range(0, src_ref.shape[0], rows)]

    def copy(t):
        r, c = tiles[t]
        return pltpu.make_async_copy(src_ref.at[pl.ds(r, rows), pl.ds(c, cols)],
                                     stage_ref.at[t % slots], sem_ref.at[t % slots])

    for t in range(min(slots, len(tiles))):
        copy(t).start()
    for t, (r, c) in enumerate(tiles):
        copy(t).wait()
        dst_ref[r:r + rows, c:c + cols] = stage_ref[t % slots].astype(BF16)
        if t + slots < len(tiles):
            copy(t + slots).start()


def _mod_kernel(c_ref, w_ref, b_ref, o_ref):
    c = c_ref[...]
    o_ref[...] = _dot(_silu(c).astype(BF16), w_ref[...].astype(BF16)) + b_ref[...]


def _modulation(c, w, b):
    batch, d = c.shape
    n = w.shape[1]
    return pl.pallas_call(
        _mod_kernel,
        grid=(n // MOD_BLOCK,),
        in_specs=[
            pl.BlockSpec((batch, d), lambda j: (0, 0)),
            pl.BlockSpec((d, MOD_BLOCK), lambda j: (0, j)),
            pl.BlockSpec((1, MOD_BLOCK), lambda j: (0, j)),
        ],
        out_specs=pl.BlockSpec((batch, MOD_BLOCK), lambda j: (0, j)),
        out_shape=jax.ShapeDtypeStruct((batch, n), F32),
        compiler_params=pltpu.CompilerParams(dimension_semantics=("arbitrary",)),
    )(c, w, b.reshape(1, n))


def _ffn_kernel(x_ref, mod_ref, nw_ref, w13_hbm, w2_hbm, *rest, mod_base, final_norm):
    o_ref, w13_ref, w2_ref, s_ref, stage13_ref, stage2_ref, sem_ref = rest[-7:]

    @pl.when(_is_first_step())
    def _():
        _load_weight_bf16(w13_hbm, w13_ref, stage13_ref, sem_ref)
        _load_weight_bf16(w2_hbm, w2_ref, stage2_ref, sem_ref)

    x = x_ref[...]
    shift, scale, gate = _mod_rows(mod_ref, mod_base)
    hb = _rms_mod(x, nw_ref[...], shift, scale).astype(BF16)

    def up(lhs, lo):
        return (_dot(lhs, w13_ref[:, lo:lo + FFN_BLOCK]),
                _dot(lhs, w13_ref[:, D_FF + lo:D_FF + lo + FFN_BLOCK]))

    for lo in range(0, D_FF, FFN_BLOCK):
        if lo == 0:
            parts = [up(hb[r:r + HEAD_ROWS], lo) for r in range(0, x.shape[0], HEAD_ROWS)]
            a = jnp.concatenate([p[0] for p in parts], axis=0)
            b = jnp.concatenate([p[1] for p in parts], axis=0)
        else:
            a, b = up(hb, lo)
        s_ref[:, lo:lo + FFN_BLOCK] = (_silu(a) * b).astype(BF16)
    for r in range(0, x.shape[0], TAIL_ROWS):
        rows = slice(r, r + TAIL_ROWS)
        out = x[rows] + gate * (0.5 * _dot(s_ref[rows, :], w2_ref[...]))
        if final_norm:
            ms = jnp.mean(out * out, axis=-1, keepdims=True)
            out = out * lax.rsqrt(ms + EPS) * rest[0][...]
        o_ref[rows, :] = out


def _ffn(x, mod, mod_base, norm_w, w13, w2, final_w):
    batch, seq, d = x.shape
    tm = FFN_TOKENS
    final_norm = final_w is not None
    in_specs = [
        pl.BlockSpec((None, tm, d), lambda b, i: (b, i, 0)),
        _resident(mod.shape),
        _resident((1, d)),
        pl.BlockSpec(memory_space=pl.ANY),
        pl.BlockSpec(memory_space=pl.ANY),
    ]
    args = [x, mod, norm_w.reshape(1, d), w13, w2]
    if final_norm:
        in_specs.append(_resident((1, d)))
        args.append(final_w.reshape(1, d))
    return pl.pallas_call(
        functools.partial(_ffn_kernel, mod_base=mod_base, final_norm=final_norm),
        grid=(batch, seq // tm),
        in_specs=in_specs,
        out_specs=pl.BlockSpec((None, tm, d), lambda b, i: (b, i, 0)),
        out_shape=jax.ShapeDtypeStruct(x.shape, F32),
        scratch_shapes=[
            pltpu.VMEM(w13.shape, BF16),
            pltpu.VMEM(w2.shape, BF16),
            pltpu.VMEM((tm, D_FF), BF16),
            pltpu.VMEM((STAGE_SLOTS, FFN_STAGE_ROWS[0], w13.shape[1]), F32),
            pltpu.VMEM((STAGE_SLOTS, FFN_STAGE_ROWS[1], w2.shape[1]), F32),
            pltpu.SemaphoreType.DMA((STAGE_SLOTS,)),
        ],
        compiler_params=pltpu.CompilerParams(
            dimension_semantics=("arbitrary", "arbitrary"),
            vmem_limit_bytes=V7X_VMEM_LIMIT_BYTES),
    )(*args)


def _rope(t, cos, sin):
    t1, t2 = t[:, :ROPE_HALF], t[:, ROPE_HALF:]
    return jnp.concatenate([t1 * cos - t2 * sin, t1 * sin + t2 * cos], axis=-1)


def _mixer_kernel(x_ref, mod_ref, nw_ref, cos_ref, sin_ref, idec_ref, qdec_ref, kdec_ref,
                  cdec_ref, w_in_hbm, gnw_ref, w_r_hbm, plin_hbm, pscale_ref, w_p_hbm,
                  w_out_hbm, o_ref, state_ref, uext_ref, gated_ref,
                  w_in_ref, w_r_ref, plin_ref, w_p_ref, w_out_ref,
                  stage_ref, stage_plin_ref, sem_ref):
    i = pl.program_id(1)
    tm = x_ref.shape[0]
    n_chunks = tm // RET_CHUNK

    @pl.when(_is_first_step())
    def _():
        _load_weight_bf16(w_in_hbm, w_in_ref, stage_ref, sem_ref)
        _load_weight_bf16(w_r_hbm, w_r_ref, stage_ref, sem_ref)
        _load_weight_bf16(w_p_hbm, w_p_ref, stage_ref, sem_ref)
        _load_weight_bf16(w_out_hbm, w_out_ref, stage_ref, sem_ref)
        _load_weight_bf16(plin_hbm, plin_ref, stage_plin_ref, sem_ref)
        for gi in range(len(POOL_WINDOWS)):
            cols = slice(gi * POOL_GROUP_W, (gi + 1) * POOL_GROUP_W)
            lin = plin_ref[cols, :].astype(F32) * pscale_ref[:, cols]
            hi = lin.astype(BF16)
            lo = (lin - hi.astype(F32)).astype(BF16)
            w_g = w_p_ref[cols, :]
            w_p_ref[cols, :] = (_dot(hi, w_g) + _dot(lo, w_g)).astype(BF16)

    @pl.when(i == 0)
    def _():
        state_ref[...] = jnp.zeros(state_ref.shape, F32)
        uext_ref[0:POOL_HALO, :] = jnp.zeros((POOL_HALO, POOL_W), F32)

    x = x_ref[...]
    shift, scale, gate = _mod_rows(mod_ref, 3)
    hb = _rms_mod(x, nw_ref[...], shift, scale).astype(BF16)
    cos = cos_ref[...]
    sin = sin_ref[...]

    def proj(off, width):
        return _dot(hb, w_in_ref[:, off:off + width])

    qk, roped, vg, scores, start_state = {}, {}, {}, {}, {}

    def proj_qk(h):
        qk[h] = (proj(OFF_Q + h * RET_DK, RET_DK), proj(OFF_K + h * RET_DK, RET_DK))

    def chunk_rows(c):
        return slice(c * RET_CHUNK, (c + 1) * RET_CHUNK)

    def rope_qk(h):
        q, k = qk.pop(h)
        qr = _rope(q, cos, sin)
        kr = _rope(k, cos, sin) * (RET_DK ** -0.5)
        qdec = jnp.concatenate([qdec_ref[h]] * (RET_DK // ROPE_HALF), axis=1)
        kdec = jnp.concatenate([kdec_ref[h]] * (RET_DK // ROPE_HALF), axis=1)
        roped[h] = [(qr[chunk_rows(c)].astype(BF16), (qr[chunk_rows(c)] * qdec).astype(BF16),
                     kr[chunk_rows(c)].astype(BF16), (kr[chunk_rows(c)] * kdec).astype(BF16))
                    for c in range(n_chunks)]

    def proj_vg(h):
        v = proj(OFF_V + h * RET_DV, RET_DV)
        vg[h] = (v.astype(BF16), proj(OFF_G + h * RET_DV, RET_DV))

    def state_step(h, c):
        _, _, _, kd = roped[h][c]
        st = state_ref[h]
        start_state[h].append(st.astype(BF16))
        upd = lax.dot_general(kd, vg[h][0][chunk_rows(c)], (((0,), (0,)), ((), ())),
                              preferred_element_type=F32)
        state_ref[h] = cdec_ref[h] * st + upd

    def retention_first(h):
        scores[h] = [lax.dot_general(qb, kb, (((1,), (1,)), ((), ())),
                                     preferred_element_type=F32)
                     for qb, _, kb, _ in roped[h]]
        start_state[h] = []
        state_step(h, 0)

    def retention_rest(h):
        for c in range(1, n_chunks):
            state_step(h, c)

    def finish_head(h):
        vb, g = vg.pop(h)
        idec = idec_ref[h]
        outs = [_dot(jnp.concatenate([(s * idec).astype(BF16), qd], axis=1),
                     jnp.concatenate([vb[chunk_rows(c)], stb], axis=0))
                for c, (s, (_, qd, _, _), stb) in enumerate(
                    zip(scores.pop(h), roped.pop(h), start_state.pop(h)))]
        o = outs[0] if n_chunks == 1 else jnp.concatenate(outs, axis=0)
        mu = jnp.mean(o, axis=-1, keepdims=True)
        oc = o - mu
        var = jnp.mean(oc * oc, axis=-1, keepdims=True)
        ret = oc * lax.rsqrt(var + EPS) * gnw_ref[:, h * RET_DV:(h + 1) * RET_DV]
        gated_ref[:, h * RET_DV:(h + 1) * RET_DV] = (_silu(g) * ret).astype(BF16)

    def pool_windows(u):
        uext_ref[POOL_HALO:POOL_HALO + tm, :] = u
        pos = (i * tm + lax.broadcasted_iota(jnp.int32, (tm, 1), 0)).astype(F32)
        pooled = []
        for gi, w in enumerate(POOL_WINDOWS):
            cols = slice(gi * POOL_GROUP_W, (gi + 1) * POOL_GROUP_W)
            s = uext_ref[:, cols]
            span = 1
            while span < w:
                s = s + pltpu.roll(s, span, axis=0)
                span *= 2
            count = jnp.minimum(pos + 1.0, float(w))
            pooled.append((s[POOL_HALO:, :] / count - u[:, cols]).astype(BF16))
        uext_ref[0:POOL_HALO, :] = uext_ref[tm:tm + POOL_HALO, :]
        return jnp.concatenate(pooled, axis=1)

    proj_qk(0)
    u = proj(OFF_U, POOL_W)
    proj_vg(0)
    rope_qk(0)
    pooled = pool_windows(u)
    for h in range(RET_HEADS):
        last = h + 1 == RET_HEADS
        if not last:
            proj_qk(h + 1)
        retention_first(h)
        if not last:
            rope_qk(h + 1)
            proj_vg(h + 1)
        else:
            a_r = proj(OFF_AR, D_MODEL)
            a_p = proj(OFF_AP, D_MODEL)
        retention_rest(h)
        finish_head(h)
    y_p = _dot(pooled, w_p_ref[...])

    chunks = [chunk_rows(c) for c in range(n_chunks)]
    y_r = [_dot(gated_ref[rows, :], w_r_ref[...]) for rows in chunks[:1]]
    for c, rows in enumerate(chunks):
        if c + 1 < n_chunks:
            y_r.append(_dot(gated_ref[chunks[c + 1], :], w_r_ref[...]))
        merged = (jax.nn.sigmoid(a_r[rows]) * y_r[c]
                  + jax.nn.sigmoid(a_p[rows]) * y_p[rows])
        o_ref[rows, :] = x[rows] + gate * _dot(merged.astype(BF16), w_out_ref[...])


def _retention_tables():
    c = RET_CHUNK
    log_gamma = np.log1p(-(2.0 ** (-5.0 - np.arange(RET_HEADS, dtype=np.float32)))).astype(np.float32)
    idx = np.arange(c, dtype=np.float32)
    diff = idx[:, None] - idx[None, :]
    inner = np.where(diff >= 0, np.exp(log_gamma[:, None, None] * np.maximum(diff, 0.0)), 0.0)
    q_decay = np.exp(log_gamma[:, None] * (idx + 1.0))
    k_decay = np.exp(log_gamma[:, None] * (c - 1.0 - idx))
    chunk_decay = np.exp(log_gamma * c)
    wide = (RET_HEADS, c, ROPE_HALF)
    return (inner.astype(np.float32),
            np.ascontiguousarray(np.broadcast_to(q_decay[:, :, None], wide), dtype=np.float32),
            np.ascontiguousarray(np.broadcast_to(k_decay[:, :, None], wide), dtype=np.float32),
            chunk_decay.astype(np.float32))


def _rope_tables(seq):
    inv = (1.0 / (ROPE_BASE ** (np.arange(ROPE_HALF, dtype=np.float32) / ROPE_HALF))).astype(np.float32)
    ang = np.arange(seq, dtype=np.float32)[:, None] * inv[None, :]
    return np.cos(ang).astype(np.float32), np.sin(ang).astype(np.float32)


def _mixer(x, mod, norm_w, w_in, gn_w, w_r, pool_lin, pool_scale, w_p, w_out):
    batch, seq, d = x.shape
    tm = MIX_TOKENS
    cos, sin = _rope_tables(seq)
    idec, qdec, kdec, cdec = _retention_tables()
    in_specs = [
        pl.BlockSpec((None, tm, d), lambda b, i: (b, i, 0)),
        _resident(mod.shape),
        _resident((1, d)),
        pl.BlockSpec((tm, ROPE_HALF), lambda b, i: (i, 0)),
        pl.BlockSpec((tm, ROPE_HALF), lambda b, i: (i, 0)),
        _resident(idec.shape),
        _resident(qdec.shape),
        _resident(kdec.shape),
        pl.BlockSpec(memory_space=pltpu.SMEM),
        pl.BlockSpec(memory_space=pl.ANY),
        _resident((1, RET_HEADS * RET_DV)),
        pl.BlockSpec(memory_space=pl.ANY),
        pl.BlockSpec(memory_space=pl.ANY),
        _resident((1, POOL_W)),
        pl.BlockSpec(memory_space=pl.ANY),
        pl.BlockSpec(memory_space=pl.ANY),
    ]
    pool_lin = pool_lin.reshape(POOL_W, POOL_GROUP_W)
    return pl.pallas_call(
        _mixer_kernel,
        grid=(batch, seq // tm),
        in_specs=in_specs,
        out_specs=pl.BlockSpec((None, tm, d), lambda b, i: (b, i, 0)),
        out_shape=jax.ShapeDtypeStruct(x.shape, F32),
        scratch_shapes=[
            pltpu.VMEM((RET_HEADS, RET_DK, RET_DV), F32),
            pltpu.VMEM((POOL_HALO + tm, POOL_W), F32),
            pltpu.VMEM((tm, RET_HEADS * RET_DV), BF16),
            pltpu.VMEM(w_in.shape, BF16),
            pltpu.VMEM(w_r.shape, BF16),
            pltpu.VMEM(pool_lin.shape, BF16),
            pltpu.VMEM(w_p.shape, BF16),
            pltpu.VMEM(w_out.shape, BF16),
            pltpu.VMEM((MIX_STAGE_SLOTS, MIX_STAGE_ROWS, d), F32),
            pltpu.VMEM((STAGE_SLOTS, POOL_STAGE_ROWS, POOL_GROUP_W), F32),
            pltpu.SemaphoreType.DMA((MIX_STAGE_SLOTS,)),
        ],
        compiler_params=pltpu.CompilerParams(
            dimension_semantics=("arbitrary", "arbitrary"),
            vmem_limit_bytes=V7X_VMEM_LIMIT_BYTES),
    )(x, mod, norm_w.reshape(1, d), cos, sin, idec, qdec, kdec, cdec, w_in,
      gn_w.reshape(1, -1), w_r, pool_lin, pool_scale.reshape(1, -1), w_p, w_out)


def kernel(x, c, ada_w, ada_b, norm_ffn1, ffn1_w13, ffn1_w2, norm_mix, w_in, ret_gn_w,
           w_ret_branch, pool_lin, pool_scale, w_pool_branch, w_out, norm_ffn2, ffn2_w13,
           ffn2_w2, norm_final):
    depth = ada_w.shape[0]
    for l in range(depth):
        mod = _modulation(c, ada_w[l], ada_b[l])
        x = _ffn(x, mod, 0, norm_ffn1[l], ffn1_w13[l], ffn1_w2[l], None)
        x = _mixer(x, mod, norm_mix[l], w_in[l], ret_gn_w[l], w_ret_branch[l], pool_lin[l],
                   pool_scale[l], w_pool_branch[l], w_out[l])
        x = _ffn(x, mod, 6, norm_ffn2[l], ffn2_w13[l], ffn2_w2[l],
                 norm_final if l == depth - 1 else None)
    return x
```

```python
import functools

import jax
import jax.numpy as jnp
import numpy as np
from jax import lax
from jax.experimental import pallas as pl
from jax.experimental.pallas import tpu as pltpu

F32 = jnp.float32
BF16 = jnp.bfloat16

D_MODEL = 1024
N_MOD = 9
EPS = 1e-6
D_FF = 2816

RET_HEADS = 4
RET_DK = 256
RET_DV = 512
RET_CHUNK = 256
ROPE_BASE = 10000.0
ROPE_HALF = RET_DK // 2

POOL_WINDOWS = (2, 4, 8, 16)
POOL_GROUP_W = 256
POOL_W = len(POOL_WINDOWS) * POOL_GROUP_W
POOL_HALO = 16

OFF_Q = 0
OFF_K = OFF_Q + RET_HEADS * RET_DK
OFF_V = OFF_K + RET_HEADS * RET_DK
OFF_G = OFF_V + RET_HEADS * RET_DV
OFF_U = OFF_G + RET_HEADS * RET_DV
OFF_AR = OFF_U + POOL_W
OFF_AP = OFF_AR + D_MODEL

V7X_VMEM_LIMIT_BYTES = 61 * 1024 * 1024
V7X_MXU_WIDTH = 256

FFN_TOKENS = 1024
HEAD_ROWS = 128
TAIL_ROWS = 256
FFN_BLOCK = V7X_MXU_WIDTH
MIX_TOKENS = 2 * RET_CHUNK
MOD_BLOCK = 2304
STAGE_SLOTS = 6
FFN_STAGE_ROWS = (64, 256)
MIX_STAGE_SLOTS = 6
MIX_STAGE_ROWS = 128
POOL_STAGE_ROWS = 64


def _dot(a, b):
    return jnp.dot(a, b, preferred_element_type=F32)


def _silu(a):
    return a * jax.nn.sigmoid(a)


def _rms_mod(x, norm_w, shift, scale):
    ms = jnp.mean(x * x, axis=-1, keepdims=True)
    y = x * lax.rsqrt(ms + EPS) * norm_w
    return y * (1.0 + scale) + shift


def _resident(shape):
    zeros = (0,) * len(shape)
    return pl.BlockSpec(shape, lambda *_: zeros, pipeline_mode=pl.Buffered(1))


def _is_first_step():
    return jnp.logical_and(pl.program_id(0) == 0, pl.program_id(1) == 0)


def _mod_rows(mod_ref, first):
    b = pl.program_id(0)
    return [mod_ref[pl.ds(b, 1), (first + k) * D_MODEL:(first + k + 1) * D_MODEL]
            for k in range(3)]


def _load_weight_bf16(src_ref, dst_ref, stage_ref, sem_ref):
    slots, rows, cols = stage_ref.shape
    assert src_ref.shape[0] % rows == 0 and src_ref.shape[1] % cols == 0
    tiles = [(r, c) for c in range(0, src_ref.shape[1], cols)
             for r in range(0, src_ref.shape[0], rows)]

    def copy(t):
        r, c = tiles[t]
        return pltpu.make_async_copy(src_ref.at[pl.ds(r, rows), pl.ds(c, cols)],
                                     stage_ref.at[t % slots], sem_ref.at[t % slots])

    for t in range(min(slots, len(tiles))):
        copy(t).start()
    for t, (r, c) in enumerate(tiles):
        copy(t).wait()
        dst_ref[r:r + rows, c:c + cols] = stage_ref[t % slots].astype(BF16)
        if t + slots < len(tiles):
            copy(t + slots).start()


def _mod_kernel(c_ref, w_ref, b_ref, o_ref):
    c = c_ref[...]
    o_ref[...] = _dot(_silu(c).astype(BF16), w_ref[...].astype(BF16)) + b_ref[...]


def _modulation(c, w, b):
    batch, d = c.shape
    n = w.shape[1]
    return pl.pallas_call(
        _mod_kernel,
        grid=(n // MOD_BLOCK,),
        in_specs=[
            pl.BlockSpec((batch, d), lambda j: (0, 0)),
            pl.BlockSpec((d, MOD_BLOCK), lambda j: (0, j)),
            pl.BlockSpec((1, MOD_BLOCK), lambda j: (0, j)),
        ],
        out_specs=pl.BlockSpec((batch, MOD_BLOCK), lambda j: (0, j)),
        out_shape=jax.ShapeDtypeStruct((batch, n), F32),
        compiler_params=pltpu.CompilerParams(dimension_semantics=("arbitrary",)),
    )(c, w, b.reshape(1, n))


def _ffn_kernel(x_ref, mod_ref, nw_ref, w13_hbm, w2_hbm, *rest, mod_base, final_norm):
    o_ref, w13_ref, w2_ref, s_ref, stage13_ref, stage2_ref, sem_ref = rest[-7:]

    @pl.when(_is_first_step())
    def _():
        _load_weight_bf16(w13_hbm, w13_ref, stage13_ref, sem_ref)
        _load_weight_bf16(w2_hbm, w2_ref, stage2_ref, sem_ref)

    x = x_ref[...]
    shift, scale, gate = _mod_rows(mod_ref, mod_base)
    hb = _rms_mod(x, nw_ref[...], shift, scale).astype(BF16)

    def up(lhs, lo):
        return (_dot(lhs, w13_ref[:, lo:lo + FFN_BLOCK]),
                _dot(lhs, w13_ref[:, D_FF + lo:D_FF + lo + FFN_BLOCK]))

    for lo in range(0, D_FF, FFN_BLOCK):
        if lo == 0:
            parts = [up(hb[r:r + HEAD_ROWS], lo) for r in range(0, x.shape[0], HEAD_ROWS)]
            a = jnp.concatenate([p[0] for p in parts], axis=0)
            b = jnp.concatenate([p[1] for p in parts], axis=0)
        else:
            a, b = up(hb, lo)
        s_ref[:, lo:lo + FFN_BLOCK] = (_silu(a) * b).astype(BF16)
    for r in range(0, x.shape[0], TAIL_ROWS):
        rows = slice(r, r + TAIL_ROWS)
        out = x[rows] + gate * (0.5 * _dot(s_ref[rows, :], w2_ref[...]))
        if final_norm:
            ms = jnp.mean(out * out, axis=-1, keepdims=True)
            out = out * lax.rsqrt(ms + EPS) * rest[0][...]
        o_ref[rows, :] = out


def _ffn(x, mod, mod_base, norm_w, w13, w2, final_w):
    batch, seq, d = x.shape
    tm = FFN_TOKENS
    final_norm = final_w is not None
    in_specs = [
        pl.BlockSpec((None, tm, d), lambda b, i: (b, i, 0)),
        _resident(mod.shape),
        _resident((1, d)),
        pl.BlockSpec(memory_space=pl.ANY),
        pl.BlockSpec(memory_space=pl.ANY),
    ]
    args = [x, mod, norm_w.reshape(1, d), w13, w2]
    if final_norm:
        in_specs.append(_resident((1, d)))
        args.append(final_w.reshape(1, d))
    return pl.pallas_call(
        functools.partial(_ffn_kernel, mod_base=mod_base, final_norm=final_norm),
        grid=(batch, seq // tm),
        in_specs=in_specs,
        out_specs=pl.BlockSpec((None, tm, d), lambda b, i: (b, i, 0)),
        out_shape=jax.ShapeDtypeStruct(x.shape, F32),
        scratch_shapes=[
            pltpu.VMEM(w13.shape, BF16),
            pltpu.VMEM(w2.shape, BF16),
            pltpu.VMEM((tm, D_FF), BF16),
            pltpu.VMEM((STAGE_SLOTS, FFN_STAGE_ROWS[0], w13.shape[1]), F32),
            pltpu.VMEM((STAGE_SLOTS, FFN_STAGE_ROWS[1], w2.shape[1]), F32),
            pltpu.SemaphoreType.DMA((STAGE_SLOTS,)),
        ],
        compiler_params=pltpu.CompilerParams(
            dimension_semantics=("arbitrary", "arbitrary"),
            vmem_limit_bytes=V7X_VMEM_LIMIT_BYTES),
    )(*args)


def _rope(t, cos, sin):
    t1, t2 = t[:, :ROPE_HALF], t[:, ROPE_HALF:]
    return jnp.concatenate([t1 * cos - t2 * sin, t1 * sin + t2 * cos], axis=-1)


def _mixer_kernel(x_ref, mod_ref, nw_ref, cos_ref, sin_ref, idec_ref, qdec_ref, kdec_ref,
                  cdec_ref, w_in_hbm, gnw_ref, w_r_hbm, plin_hbm, pscale_ref, w_p_hbm,
                  w_out_hbm, o_ref, state_ref, uext_ref, gated_ref,
                  w_in_ref, w_r_ref, plin_ref, w_p_ref, w_out_ref,
                  stage_ref, stage_plin_ref, sem_ref):
    i = pl.program_id(1)
    tm = x_ref.shape[0]
    n_chunks = tm // RET_CHUNK

    @pl.when(_is_first_step())
    def _():
        _load_weight_bf16(w_in_hbm, w_in_ref, stage_ref, sem_ref)
        _load_weight_bf16(w_r_hbm, w_r_ref, stage_ref, sem_ref)
        _load_weight_bf16(w_p_hbm, w_p_ref, stage_ref, sem_ref)
        _load_weight_bf16(w_out_hbm, w_out_ref, stage_ref, sem_ref)
        _load_weight_bf16(plin_hbm, plin_ref, stage_plin_ref, sem_ref)
        for gi in range(len(POOL_WINDOWS)):
            cols = slice(gi * POOL_GROUP_W, (gi + 1) * POOL_GROUP_W)
            lin = plin_ref[cols, :].astype(F32) * pscale_ref[:, cols]
            hi = lin.astype(BF16)
            lo = (lin - hi.astype(F32)).astype(BF16)
            w_g = w_p_ref[cols, :]
            w_p_ref[cols, :] = (_dot(hi, w_g) + _dot(lo, w_g)).astype(BF16)

    @pl.when(i == 0)
    def _():
        state_ref[...] = jnp.zeros(state_ref.shape, F32)
        uext_ref[0:POOL_HALO, :] = jnp.zeros((POOL_HALO, POOL_W), F32)

    x = x_ref[...]
    shift, scale, gate = _mod_rows(mod_ref, 3)
    hb = _rms_mod(x, nw_ref[...], shift, scale).astype(BF16)
    cos = cos_ref[...]
    sin = sin_ref[...]

    def proj(off, width):
        return _dot(hb, w_in_ref[:, off:off + width])

    qk, roped, vg, scores, start_state = {}, {}, {}, {}, {}

    def proj_qk(h):
        qk[h] = (proj(OFF_Q + h * RET_DK, RET_DK), proj(OFF_K + h * RET_DK, RET_DK))

    def chunk_rows(c):
        return slice(c * RET_CHUNK, (c + 1) * RET_CHUNK)

    def rope_qk(h):
        q, k = qk.pop(h)
        qr = _rope(q, cos, sin)
        kr = _rope(k, cos, sin) * (RET_DK ** -0.5)
        qdec = jnp.concatenate([qdec_ref[h]] * (RET_DK // ROPE_HALF), axis=1)
        kdec = jnp.concatenate([kdec_ref[h]] * (RET_DK // ROPE_HALF), axis=1)
        roped[h] = [(qr[chunk_rows(c)].astype(BF16), (qr[chunk_rows(c)] * qdec).astype(BF16),
                     kr[chunk_rows(c)].astype(BF16), (kr[chunk_rows(c)] * kdec).astype(BF16))
                    for c in range(n_chunks)]

    def proj_vg(h):
        v = proj(OFF_V + h * RET_DV, RET_DV)
        vg[h] = (v.astype(BF16), proj(OFF_G + h * RET_DV, RET_DV))

    def state_step(h, c):
        _, _, _, kd = roped[h][c]
        st = state_ref[h]
        start_state[h].append(st.astype(BF16))
        upd = lax.dot_general(kd, vg[h][0][chunk_rows(c)], (((0,), (0,)), ((), ())),
                              preferred_element_type=F32)
        state_ref[h] = cdec_ref[h] * st + upd

    def retention_first(h):
        scores[h] = [lax.dot_general(qb, kb, (((1,), (1,)), ((), ())),
                                     preferred_element_type=F32)
                     for qb, _, kb, _ in roped[h]]
        start_state[h] = []
        state_step(h, 0)

    def retention_rest(h):
        for c in range(1, n_chunks):
            state_step(h, c)

    def finish_head(h):
        vb, g = vg.pop(h)
        idec = idec_ref[h]
        outs = [_dot(jnp.concatenate([(s * idec).astype(BF16), qd], axis=1),
                     jnp.concatenate([vb[chunk_rows(c)], stb], axis=0))
                for c, (s, (_, qd, _, _), stb) in enumerate(
                    zip(scores.pop(h), roped.pop(h), start_state.pop(h)))]
        o = outs[0] if n_chunks == 1 else jnp.concatenate(outs, axis=0)
        mu = jnp.mean(o, axis=-1, keepdims=True)
        oc = o - mu
        var = jnp.mean(oc * oc, axis=-1, keepdims=True)
        ret = oc * lax.rsqrt(var + EPS) * gnw_ref[:, h * RET_DV:(h + 1) * RET_DV]
        gated_ref[:, h * RET_DV:(h + 1) * RET_DV] = (_silu(g) * ret).astype(BF16)

    def pool_windows(u):
        uext_ref[POOL_HALO:POOL_HALO + tm, :] = u
        pos = (i * tm + lax.broadcasted_iota(jnp.int32, (tm, 1), 0)).astype(F32)
        pooled = []
        for gi, w in enumerate(POOL_WINDOWS):
            cols = slice(gi * POOL_GROUP_W, (gi + 1) * POOL_GROUP_W)
            s = uext_ref[:, cols]
            span = 1
            while span < w:
                s = s + pltpu.roll(s, span, axis=0)
                span *= 2
            count = jnp.minimum(pos + 1.0, float(w))
            pooled.append((s[POOL_HALO:, :] / count - u[:, cols]).astype(BF16))
        uext_ref[0:POOL_HALO, :] = uext_ref[tm:tm + POOL_HALO, :]
        return jnp.concatenate(pooled, axis=1)

    proj_qk(0)
    u = proj(OFF_U, POOL_W)
    proj_vg(0)
    rope_qk(0)
    pooled = pool_windows(u)
    for h in range(RET_HEADS):
        last = h + 1 == RET_HEADS
        if not last:
            proj_qk(h + 1)
        retention_first(h)
        if not last:
            rope_qk(h + 1)
            proj_vg(h + 1)
        else:
            a_r = proj(OFF_AR, D_MODEL)
            a_p = proj(OFF_AP, D_MODEL)
        retention_rest(h)
        finish_head(h)
    y_p = _dot(pooled, w_p_ref[...])

    chunks = [chunk_rows(c) for c in range(n_chunks)]
    y_r = [_dot(gated_ref[rows, :], w_r_ref[...]) for rows in chunks[:1]]
    for c, rows in enumerate(chunks):
        if c + 1 < n_chunks:
            y_r.append(_dot(gated_ref[chunks[c + 1], :], w_r_ref[...]))
        merged = (jax.nn.sigmoid(a_r[rows]) * y_r[c]
                  + jax.nn.sigmoid(a_p[rows]) * y_p[rows])
        o_ref[rows, :] = x[rows] + gate * _dot(merged.astype(BF16), w_out_ref[...])


def _retention_tables():
    c = RET_CHUNK
    log_gamma = np.log1p(-(2.0 ** (-5.0 - np.arange(RET_HEADS, dtype=np.float32)))).astype(np.float32)
    idx = np.arange(c, dtype=np.float32)
    diff = idx[:, None] - idx[None, :]
    inner = np.where(diff >= 0, np.exp(log_gamma[:, None, None] * np.maximum(diff, 0.0)), 0.0)
    q_decay = np.exp(log_gamma[:, None] * (idx + 1.0))
    k_decay = np.exp(log_gamma[:, None] * (c - 1.0 - idx))
    chunk_decay = np.exp(log_gamma * c)
    wide = (RET_HEADS, c, ROPE_HALF)
    return (inner.astype(np.float32),
            np.ascontiguousarray(np.broadcast_to(q_decay[:, :, None], wide), dtype=np.float32),
            np.ascontiguousarray(np.broadcast_to(k_decay[:, :, None], wide), dtype=np.float32),
            chunk_decay.astype(np.float32))


def _rope_tables(seq):
    inv = (1.0 / (ROPE_BASE ** (np.arange(ROPE_HALF, dtype=np.float32) / ROPE_HALF))).astype(np.float32)
    ang = np.arange(seq, dtype=np.float32)[:, None] * inv[None, :]
    return np.cos(ang).astype(np.float32), np.sin(ang).astype(np.float32)


def _mixer(x, mod, norm_w, w_in, gn_w, w_r, pool_lin, pool_scale, w_p, w_out):
    batch, seq, d = x.shape
    tm = MIX_TOKENS
    cos, sin = _rope_tables(seq)
    idec, qdec, kdec, cdec = _retention_tables()
    in_specs = [
        pl.BlockSpec((None, tm, d), lambda b, i: (b, i, 0)),
        _resident(mod.shape),
        _resident((1, d)),
        pl.BlockSpec((tm, ROPE_HALF), lambda b, i: (i, 0)),
        pl.BlockSpec((tm, ROPE_HALF), lambda b, i: (i, 0)),
        _resident(idec.shape),
        _resident(qdec.shape),
        _resident(kdec.shape),
        pl.BlockSpec(memory_space=pltpu.SMEM),
        pl.BlockSpec(memory_space=pl.ANY),
        _resident((1, RET_HEADS * RET_DV)),
        pl.BlockSpec(memory_space=pl.ANY),
        pl.BlockSpec(memory_space=pl.ANY),
        _resident((1, POOL_W)),
        pl.BlockSpec(memory_space=pl.ANY),
        pl.BlockSpec(memory_space=pl.ANY),
    ]
    pool_lin = pool_lin.reshape(POOL_W, POOL_GROUP_W)
    return pl.pallas_call(
        _mixer_kernel,
        grid=(batch, seq // tm),
        in_specs=in_specs,
        out_specs=pl.BlockSpec((None, tm, d), lambda b, i: (b, i, 0)),
        out_shape=jax.ShapeDtypeStruct(x.shape, F32),
        scratch_shapes=[
            pltpu.VMEM((RET_HEADS, RET_DK, RET_DV), F32),
            pltpu.VMEM((POOL_HALO + tm, POOL_W), F32),
            pltpu.VMEM((tm, RET_HEADS * RET_DV), BF16),
            pltpu.VMEM(w_in.shape, BF16),
            pltpu.VMEM(w_r.shape, BF16),
            pltpu.VMEM(pool_lin.shape, BF16),
            pltpu.VMEM(w_p.shape, BF16),
            pltpu.VMEM(w_out.shape, BF16),
            pltpu.VMEM((MIX_STAGE_SLOTS, MIX_STAGE_ROWS, d), F32),
            pltpu.VMEM((STAGE_SLOTS, POOL_STAGE_ROWS, POOL_GROUP_W), F32),
            pltpu.SemaphoreType.DMA((MIX_STAGE_SLOTS,)),
        ],
        compiler_params=pltpu.CompilerParams(
            dimension_semantics=("arbitrary", "arbitrary"),
            vmem_limit_bytes=V7X_VMEM_LIMIT_BYTES),
    )(x, mod, norm_w.reshape(1, d), cos, sin, idec, qdec, kdec, cdec, w_in,
      gn_w.reshape(1, -1), w_r, pool_lin, pool_scale.reshape(1, -1), w_p, w_out)


def kernel(x, c, ada_w, ada_b, norm_ffn1, ffn1_w13, ffn1_w2, norm_mix, w_in, ret_gn_w,
           w_ret_branch, pool_lin, pool_scale, w_pool_branch, w_out, norm_ffn2, ffn2_w13,
           ffn2_w2, norm_final):
    depth = ada_w.shape[0]
    assert x.shape[1] % FFN_TOKENS == 0 and x.shape[1] % MIX_TOKENS == 0
    assert x.shape[2] == D_MODEL and ada_w.shape[2] == N_MOD * D_MODEL
    for l in range(depth):
        mod = _modulation(c, ada_w[l], ada_b[l])
        x = _ffn(x, mod, 0, norm_ffn1[l], ffn1_w13[l], ffn1_w2[l], None)
        x = _mixer(x, mod, norm_mix[l], w_in[l], ret_gn_w[l], w_ret_branch[l], pool_lin[l],
                   pool_scale[l], w_pool_branch[l], w_out[l])
        x = _ffn(x, mod, 6, norm_ffn2[l], ffn2_w13[l], ffn2_w2[l],
                 norm_final if l == depth - 1 else None)
    return x
```

```python
import functools

import jax
import jax.numpy as jnp
import numpy as np
from jax import lax
from jax.experimental import pallas as pl
from jax.experimental.pallas import tpu as pltpu

F32 = jnp.float32
BF16 = jnp.bfloat16

D_MODEL = 1024
N_MOD = 9
EPS = 1e-6
D_FF = 2816

RET_HEADS = 4
RET_DK = 256
RET_DV = 512
RET_CHUNK = 256
ROPE_BASE = 10000.0
ROPE_HALF = RET_DK // 2

POOL_WINDOWS = (2, 4, 8, 16)
POOL_GROUP_W = 256
POOL_W = len(POOL_WINDOWS) * POOL_GROUP_W
POOL_HALO = 16

OFF_Q = 0
OFF_K = OFF_Q + RET_HEADS * RET_DK
OFF_V = OFF_K + RET_HEADS * RET_DK
OFF_G = OFF_V + RET_HEADS * RET_DV
OFF_U = OFF_G + RET_HEADS * RET_DV
OFF_AR = OFF_U + POOL_W
OFF_AP = OFF_AR + D_MODEL

V7X_VMEM_LIMIT_BYTES = 61 * 1024 * 1024
V7X_MXU_WIDTH = 256

FFN_TOKENS = 1024
HEAD_ROWS = 128
TAIL_ROWS = 256
FFN_BLOCK = V7X_MXU_WIDTH
MIX_TOKENS = 2 * RET_CHUNK
MOD_BLOCK = 2304
STAGE_SLOTS = 6
FFN_STAGE_ROWS = (64, 256)
MIX_STAGE_SLOTS = 5
MIX_STAGE_ROWS = 128
POOL_STAGE_ROWS = 64


def _dot(a, b):
    return jnp.dot(a, b, preferred_element_type=F32)


def _silu(a):
    return a * jax.nn.sigmoid(a)


def _rms_mod(x, norm_w, shift, scale):
    ms = jnp.mean(x * x, axis=-1, keepdims=True)
    y = x * lax.rsqrt(ms + EPS) * norm_w
    return y * (1.0 + scale) + shift


def _resident(shape):
    zeros = (0,) * len(shape)
    return pl.BlockSpec(shape, lambda *_: zeros, pipeline_mode=pl.Buffered(1))


def _is_first_step():
    return jnp.logical_and(pl.program_id(0) == 0, pl.program_id(1) == 0)


def _mod_rows(mod_ref, first):
    b = pl.program_id(0)
    return [mod_ref[pl.ds(b, 1), (first + k) * D_MODEL:(first + k + 1) * D_MODEL]
            for k in range(3)]


def _load_weight_bf16(src_ref, dst_ref, stage_ref, sem_ref):
    slots, rows, cols = stage_ref.shape
    assert src_ref.shape[0] % rows == 0 and src_ref.shape[1] % cols == 0
    tiles = [(r, c) for c in range(0, src_ref.shape[1], cols)
             for r in range(0, src_ref.shape[0], rows)]

    def copy(t):
        r, c = tiles[t]
        return pltpu.make_async_copy(src_ref.at[pl.ds(r, rows), pl.ds(c, cols)],
                                     stage_ref.at[t % slots], sem_ref.at[t % slots])

    for t in range(min(slots, len(tiles))):
        copy(t).start()
    for t, (r, c) in enumerate(tiles):
        copy(t).wait()
        dst_ref[r:r + rows, c:c + cols] = stage_ref[t % slots].astype(BF16)
        if t + slots < len(tiles):
            copy(t + slots).start()


def _mod_kernel(c_ref, w_ref, b_ref, o_ref):
    c = c_ref[...]
    o_ref[...] = _dot(_silu(c).astype(BF16), w_ref[...].astype(BF16)) + b_ref[...]


def _modulation(c, w, b):
    batch, d = c.shape
    n = w.shape[1]
    return pl.pallas_call(
        _mod_kernel,
        grid=(n // MOD_BLOCK,),
        in_specs=[
            pl.BlockSpec((batch, d), lambda j: (0, 0)),
            pl.BlockSpec((d, MOD_BLOCK), lambda j: (0, j)),
            pl.BlockSpec((1, MOD_BLOCK), lambda j: (0, j)),
        ],
        out_specs=pl.BlockSpec((batch, MOD_BLOCK), lambda j: (0, j)),
        out_shape=jax.ShapeDtypeStruct((batch, n), F32),
        compiler_params=pltpu.CompilerParams(dimension_semantics=("arbitrary",)),
    )(c, w, b.reshape(1, n))


def _ffn_kernel(x_ref, mod_ref, nw_ref, w13_hbm, w2_hbm, *rest, mod_base, final_norm):
    o_ref, w13_ref, w2_ref, s_ref, stage13_ref, stage2_ref, sem_ref = rest[-7:]

    @pl.when(_is_first_step())
    def _():
        _load_weight_bf16(w13_hbm, w13_ref, stage13_ref, sem_ref)
        _load_weight_bf16(w2_hbm, w2_ref, stage2_ref, sem_ref)

    x = x_ref[...]
    shift, scale, gate = _mod_rows(mod_ref, mod_base)
    hb = _rms_mod(x, nw_ref[...], shift, scale).astype(BF16)

    def up(lhs, lo):
        return (_dot(lhs, w13_ref[:, lo:lo + FFN_BLOCK]),
                _dot(lhs, w13_ref[:, D_FF + lo:D_FF + lo + FFN_BLOCK]))

    for lo in range(0, D_FF, FFN_BLOCK):
        if lo == 0:
            parts = [up(hb[r:r + HEAD_ROWS], lo) for r in range(0, x.shape[0], HEAD_ROWS)]
            a = jnp.concatenate([p[0] for p in parts], axis=0)
            b = jnp.concatenate([p[1] for p in parts], axis=0)
        else:
            a, b = up(hb, lo)
        s_ref[:, lo:lo + FFN_BLOCK] = (_silu(a) * b).astype(BF16)
    for r in range(0, x.shape[0], TAIL_ROWS):
        rows = slice(r, r + TAIL_ROWS)
        out = x[rows] + gate * (0.5 * _dot(s_ref[rows, :], w2_ref[...]))
        if final_norm:
            ms = jnp.mean(out * out, axis=-1, keepdims=True)
            out = out * lax.rsqrt(ms + EPS) * rest[0][...]
        o_ref[rows, :] = out


def _ffn(x, mod, mod_base, norm_w, w13, w2, final_w):
    batch, seq, d = x.shape
    tm = FFN_TOKENS
    final_norm = final_w is not None
    in_specs = [
        pl.BlockSpec((None, tm, d), lambda b, i: (b, i, 0)),
        _resident(mod.shape),
        _resident((1, d)),
        pl.BlockSpec(memory_space=pl.ANY),
        pl.BlockSpec(memory_space=pl.ANY),
    ]
    args = [x, mod, norm_w.reshape(1, d), w13, w2]
    if final_norm:
        in_specs.append(_resident((1, d)))
        args.append(final_w.reshape(1, d))
    return pl.pallas_call(
        functools.partial(_ffn_kernel, mod_base=mod_base, final_norm=final_norm),
        grid=(batch, seq // tm),
        in_specs=in_specs,
        out_specs=pl.BlockSpec((None, tm, d), lambda b, i: (b, i, 0)),
        out_shape=jax.ShapeDtypeStruct(x.shape, F32),
        scratch_shapes=[
            pltpu.VMEM(w13.shape, BF16),
            pltpu.VMEM(w2.shape, BF16),
            pltpu.VMEM((tm, D_FF), BF16),
            pltpu.VMEM((STAGE_SLOTS, FFN_STAGE_ROWS[0], w13.shape[1]), F32),
            pltpu.VMEM((STAGE_SLOTS, FFN_STAGE_ROWS[1], w2.shape[1]), F32),
            pltpu.SemaphoreType.DMA((STAGE_SLOTS,)),
        ],
        compiler_params=pltpu.CompilerParams(
            dimension_semantics=("arbitrary", "arbitrary"),
            vmem_limit_bytes=V7X_VMEM_LIMIT_BYTES),
    )(*args)


def _rope(t, cos, sin):
    t1, t2 = t[:, :ROPE_HALF], t[:, ROPE_HALF:]
    return jnp.concatenate([t1 * cos - t2 * sin, t1 * sin + t2 * cos], axis=-1)


def _mixer_kernel(x_ref, mod_ref, nw_ref, cos_ref, sin_ref, idec_ref, qdec_ref, kdec_ref,
                  cdec_ref, w_in_hbm, gnw_ref, w_r_hbm, plin_hbm, pscale_ref, w_p_hbm,
                  w_out_hbm, o_ref, state_ref, uext_ref, gated_ref,
                  w_in_ref, w_r_ref, plin_ref, w_p_ref, w_out_ref,
                  stage_ref, stage_plin_ref, sem_ref):
    i = pl.program_id(1)
    tm = x_ref.shape[0]
    n_chunks = tm // RET_CHUNK

    @pl.when(_is_first_step())
    def _():
        _load_weight_bf16(w_in_hbm, w_in_ref, stage_ref, sem_ref)
        _load_weight_bf16(w_r_hbm, w_r_ref, stage_ref, sem_ref)
        _load_weight_bf16(w_p_hbm, w_p_ref, stage_ref, sem_ref)
        _load_weight_bf16(w_out_hbm, w_out_ref, stage_ref, sem_ref)
        _load_weight_bf16(plin_hbm, plin_ref, stage_plin_ref, sem_ref)
        for gi in range(len(POOL_WINDOWS)):
            cols = slice(gi * POOL_GROUP_W, (gi + 1) * POOL_GROUP_W)
            lin = plin_ref[cols, :].astype(F32) * pscale_ref[:, cols]
            hi = lin.astype(BF16)
            lo = (lin - hi.astype(F32)).astype(BF16)
            w_g = w_p_ref[cols, :]
            w_p_ref[cols, :] = (_dot(hi, w_g) + _dot(lo, w_g)).astype(BF16)

    @pl.when(i == 0)
    def _():
        state_ref[...] = jnp.zeros(state_ref.shape, F32)
        uext_ref[0:POOL_HALO, :] = jnp.zeros((POOL_HALO, POOL_W), F32)

    x = x_ref[...]
    shift, scale, gate = _mod_rows(mod_ref, 3)
    hb = _rms_mod(x, nw_ref[...], shift, scale).astype(BF16)
    tile_rows = pl.ds(pl.multiple_of(i * tm, tm), tm)
    cos = cos_ref[tile_rows, :]
    sin = sin_ref[tile_rows, :]

    def proj(off, width):
        return _dot(hb, w_in_ref[:, off:off + width])

    qk, roped, vg, scores, start_state = {}, {}, {}, {}, {}

    def proj_qk(h):
        qk[h] = (proj(OFF_Q + h * RET_DK, RET_DK), proj(OFF_K + h * RET_DK, RET_DK))

    def chunk_rows(c):
        return slice(c * RET_CHUNK, (c + 1) * RET_CHUNK)

    def rope_qk(h):
        q, k = qk.pop(h)
        qr = _rope(q, cos, sin)
        kr = _rope(k, cos, sin) * (RET_DK ** -0.5)
        qdec = jnp.concatenate([qdec_ref[h]] * (RET_DK // ROPE_HALF), axis=1)
        kdec = jnp.concatenate([kdec_ref[h]] * (RET_DK // ROPE_HALF), axis=1)
        roped[h] = [(qr[chunk_rows(c)].astype(BF16), (qr[chunk_rows(c)] * qdec).astype(BF16),
                     kr[chunk_rows(c)].astype(BF16), (kr[chunk_rows(c)] * kdec).astype(BF16))
                    for c in range(n_chunks)]

    def proj_vg(h):
        v = proj(OFF_V + h * RET_DV, RET_DV)
        vg[h] = (v.astype(BF16), proj(OFF_G + h * RET_DV, RET_DV))

    def state_step(h, c):
        _, _, _, kd = roped[h][c]
        st = state_ref[h]
        start_state[h].append(st.astype(BF16))
        upd = lax.dot_general(kd, vg[h][0][chunk_rows(c)], (((0,), (0,)), ((), ())),
                              preferred_element_type=F32)
        state_ref[h] = cdec_ref[h] * st + upd

    def retention_first(h):
        scores[h] = [lax.dot_general(qb, kb, (((1,), (1,)), ((), ())),
                                     preferred_element_type=F32)
                     for qb, _, kb, _ in roped[h]]
        start_state[h] = []
        state_step(h, 0)

    def retention_rest(h):
        for c in range(1, n_chunks):
            state_step(h, c)

    def finish_head(h):
        vb, g = vg.pop(h)
        idec = idec_ref[h]
        outs = [_dot(jnp.concatenate([(s * idec).astype(BF16), qd], axis=1),
                     jnp.concatenate([vb[chunk_rows(c)], stb], axis=0))
                for c, (s, (_, qd, _, _), stb) in enumerate(
                    zip(scores.pop(h), roped.pop(h), start_state.pop(h)))]
        o = outs[0] if n_chunks == 1 else jnp.concatenate(outs, axis=0)
        mu = jnp.mean(o, axis=-1, keepdims=True)
        oc = o - mu
        var = jnp.mean(oc * oc, axis=-1, keepdims=True)
        ret = oc * lax.rsqrt(var + EPS) * gnw_ref[:, h * RET_DV:(h + 1) * RET_DV]
        gated_ref[:, h * RET_DV:(h + 1) * RET_DV] = (_silu(g) * ret).astype(BF16)

    def pool_windows(u):
        uext_ref[POOL_HALO:POOL_HALO + tm, :] = u
        pos = (i * tm + lax.broadcasted_iota(jnp.int32, (tm, 1), 0)).astype(F32)
        pooled = []
        for gi, w in enumerate(POOL_WINDOWS):
            cols = slice(gi * POOL_GROUP_W, (gi + 1) * POOL_GROUP_W)
            s = uext_ref[:, cols]
            span = 1
            while span < w:
                s = s + pltpu.roll(s, span, axis=0)
                span *= 2
            count = jnp.minimum(pos + 1.0, float(w))
            pooled.append((s[POOL_HALO:, :] / count - u[:, cols]).astype(BF16))
        uext_ref[0:POOL_HALO, :] = uext_ref[tm:tm + POOL_HALO, :]
        return jnp.concatenate(pooled, axis=1)

    proj_qk(0)
    u = proj(OFF_U, POOL_W)
    proj_vg(0)
    rope_qk(0)
    pooled = pool_windows(u)
    for h in range(RET_HEADS):
        last = h + 1 == RET_HEADS
        if not last:
            proj_qk(h + 1)
        retention_first(h)
        if not last:
            rope_qk(h + 1)
            proj_vg(h + 1)
        else:
            a_r = proj(OFF_AR, D_MODEL)
            a_p = proj(OFF_AP, D_MODEL)
        retention_rest(h)
        finish_head(h)
    y_p = _dot(pooled, w_p_ref[...])

    chunks = [chunk_rows(c) for c in range(n_chunks)]
    y_r = [_dot(gated_ref[rows, :], w_r_ref[...]) for rows in chunks[:1]]
    for c, rows in enumerate(chunks):
        if c + 1 < n_chunks:
            y_r.append(_dot(gated_ref[chunks[c + 1], :], w_r_ref[...]))
        merged = (jax.nn.sigmoid(a_r[rows]) * y_r[c]
                  + jax.nn.sigmoid(a_p[rows]) * y_p[rows])
        o_ref[rows, :] = x[rows] + gate * _dot(merged.astype(BF16), w_out_ref[...])


def _retention_tables():
    c = RET_CHUNK
    log_gamma = np.log1p(-(2.0 ** (-5.0 - np.arange(RET_HEADS, dtype=np.float32)))).astype(np.float32)
    idx = np.arange(c, dtype=np.float32)
    diff = idx[:, None] - idx[None, :]
    inner = np.where(diff >= 0, np.exp(log_gamma[:, None, None] * np.maximum(diff, 0.0)), 0.0)
    q_decay = np.exp(log_gamma[:, None] * (idx + 1.0))
    k_decay = np.exp(log_gamma[:, None] * (c - 1.0 - idx))
    chunk_decay = np.exp(log_gamma * c)
    wide = (RET_HEADS, c, ROPE_HALF)
    return (inner.astype(np.float32),
            np.ascontiguousarray(np.broadcast_to(q_decay[:, :, None], wide), dtype=np.float32),
            np.ascontiguousarray(np.broadcast_to(k_decay[:, :, None], wide), dtype=np.float32),
            chunk_decay.astype(np.float32))


def _rope_tables(seq):
    inv = (1.0 / (ROPE_BASE ** (np.arange(ROPE_HALF, dtype=np.float32) / ROPE_HALF))).astype(np.float32)
    ang = np.arange(seq, dtype=np.float32)[:, None] * inv[None, :]
    return np.cos(ang).astype(np.float32), np.sin(ang).astype(np.float32)


def _mixer(x, mod, norm_w, w_in, gn_w, w_r, pool_lin, pool_scale, w_p, w_out):
    batch, seq, d = x.shape
    tm = MIX_TOKENS
    cos, sin = _rope_tables(seq)
    idec, qdec, kdec, cdec = _retention_tables()
    in_specs = [
        pl.BlockSpec((None, tm, d), lambda b, i: (b, i, 0)),
        _resident(mod.shape),
        _resident((1, d)),
        _resident(cos.shape),
        _resident(sin.shape),
        _resident(idec.shape),
        _resident(qdec.shape),
        _resident(kdec.shape),
        pl.BlockSpec(memory_space=pltpu.SMEM),
        pl.BlockSpec(memory_space=pl.ANY),
        _resident((1, RET_HEADS * RET_DV)),
        pl.BlockSpec(memory_space=pl.ANY),
        pl.BlockSpec(memory_space=pl.ANY),
        _resident((1, POOL_W)),
        pl.BlockSpec(memory_space=pl.ANY),
        pl.BlockSpec(memory_space=pl.ANY),
    ]
    pool_lin = pool_lin.reshape(POOL_W, POOL_GROUP_W)
    return pl.pallas_call(
        _mixer_kernel,
        grid=(batch, seq // tm),
        in_specs=in_specs,
        out_specs=pl.BlockSpec((None, tm, d), lambda b, i: (b, i, 0)),
        out_shape=jax.ShapeDtypeStruct(x.shape, F32),
        scratch_shapes=[
            pltpu.VMEM((RET_HEADS, RET_DK, RET_DV), F32),
            pltpu.VMEM((POOL_HALO + tm, POOL_W), F32),
            pltpu.VMEM((tm, RET_HEADS * RET_DV), BF16),
            pltpu.VMEM(w_in.shape, BF16),
            pltpu.VMEM(w_r.shape, BF16),
            pltpu.VMEM(pool_lin.shape, BF16),
            pltpu.VMEM(w_p.shape, BF16),
            pltpu.VMEM(w_out.shape, BF16),
            pltpu.VMEM((MIX_STAGE_SLOTS, MIX_STAGE_ROWS, d), F32),
            pltpu.VMEM((STAGE_SLOTS, POOL_STAGE_ROWS, POOL_GROUP_W), F32),
            pltpu.SemaphoreType.DMA((max(MIX_STAGE_SLOTS, STAGE_SLOTS),)),
        ],
        compiler_params=pltpu.CompilerParams(
            dimension_semantics=("arbitrary", "arbitrary"),
            vmem_limit_bytes=V7X_VMEM_LIMIT_BYTES),
    )(x, mod, norm_w.reshape(1, d), cos, sin, idec, qdec, kdec, cdec, w_in,
      gn_w.reshape(1, -1), w_r, pool_lin, pool_scale.reshape(1, -1), w_p, w_out)


def kernel(x, c, ada_w, ada_b, norm_ffn1, ffn1_w13, ffn1_w2, norm_mix, w_in, ret_gn_w,
           w_ret_branch, pool_lin, pool_scale, w_pool_branch, w_out, norm_ffn2, ffn2_w13,
           ffn2_w2, norm_final):
    depth = ada_w.shape[0]
    assert x.shape[1] % FFN_TOKENS == 0 and x.shape[1] % MIX_TOKENS == 0
    assert x.shape[2] == D_MODEL and ada_w.shape[2] == N_MOD * D_MODEL
    for l in range(depth):
        mod = _modulation(c, ada_w[l], ada_b[l])
        x = _ffn(x, mod, 0, norm_ffn1[l], ffn1_w13[l], ffn1_w2[l], None)
        x = _mixer(x, mod, norm_mix[l], w_in[l], ret_gn_w[l], w_ret_branch[l], pool_lin[l],
                   pool_scale[l], w_pool_branch[l], w_out[l])
        x = _ffn(x, mod, 6, norm_ffn2[l], ffn2_w13[l], ffn2_w2[l],
                 norm_final if l == depth - 1 else None)
    return x
```

```python
import functools

import jax
import jax.numpy as jnp
import numpy as np
from jax import lax
from jax.experimental import pallas as pl
from jax.experimental.pallas import tpu as pltpu

F32 = jnp.float32
BF16 = jnp.bfloat16

D_MODEL = 1024
N_MOD = 9
EPS = 1e-6
D_FF = 2816

RET_HEADS = 4
RET_DK = 256
RET_DV = 512
RET_CHUNK = 256
ROPE_BASE = 10000.0
ROPE_HALF = RET_DK // 2

POOL_WINDOWS = (2, 4, 8, 16)
POOL_GROUP_W = 256
POOL_W = len(POOL_WINDOWS) * POOL_GROUP_W
POOL_HALO = 16

OFF_Q = 0
OFF_K = OFF_Q + RET_HEADS * RET_DK
OFF_V = OFF_K + RET_HEADS * RET_DK
OFF_G = OFF_V + RET_HEADS * RET_DV
OFF_U = OFF_G + RET_HEADS * RET_DV
OFF_AR = OFF_U + POOL_W
OFF_AP = OFF_AR + D_MODEL

V7X_VMEM_LIMIT_BYTES = 61 * 1024 * 1024
V7X_MXU_WIDTH = 256

FFN_TOKENS = 1024
HEAD_ROWS = 128
TAIL_ROWS = 256
FFN_BLOCK = V7X_MXU_WIDTH
MIX_TOKENS = 2 * RET_CHUNK
MOD_BLOCK = 2304
STAGE_SLOTS = 6
FFN_STAGE_ROWS = (64, 256)
MIX_STAGE_SLOTS = 8
MIX_STAGE_ROWS = 128
POOL_STAGE_ROWS = 64


def _dot(a, b):
    return jnp.dot(a, b, preferred_element_type=F32)


def _silu(a):
    return a * jax.nn.sigmoid(a)


def _rms_mod(x, norm_w, shift, scale):
    ms = jnp.mean(x * x, axis=-1, keepdims=True)
    y = x * lax.rsqrt(ms + EPS) * norm_w
    return y * (1.0 + scale) + shift


def _resident(shape):
    zeros = (0,) * len(shape)
    return pl.BlockSpec(shape, lambda *_: zeros, pipeline_mode=pl.Buffered(1))


def _is_first_step():
    return jnp.logical_and(pl.program_id(0) == 0, pl.program_id(1) == 0)


def _mod_rows(mod_ref, first):
    b = pl.program_id(0)
    return [mod_ref[pl.ds(b, 1), (first + k) * D_MODEL:(first + k + 1) * D_MODEL]
            for k in range(3)]


def _load_weight_bf16(src_ref, dst_ref, stage_ref, sem_ref):
    slots, rows, cols = stage_ref.shape
    assert src_ref.shape[0] % rows == 0 and src_ref.shape[1] % cols == 0
    tiles = [(r, c) for c in range(0, src_ref.shape[1], cols)
             for r in range(0, src_ref.shape[0], rows)]

    def copy(t):
        r, c = tiles[t]
        return pltpu.make_async_copy(src_ref.at[pl.ds(r, rows), pl.ds(c, cols)],
                                     stage_ref.at[t % slots], sem_ref.at[t % slots])

    for t in range(min(slots, len(tiles))):
        copy(t).start()
    for t, (r, c) in enumerate(tiles):
        copy(t).wait()
        dst_ref[r:r + rows, c:c + cols] = stage_ref[t % slots].astype(BF16)
        if t + slots < len(tiles):
            copy(t + slots).start()


def _mod_kernel(c_ref, w_ref, b_ref, o_ref):
    c = c_ref[...]
    o_ref[...] = _dot(_silu(c).astype(BF16), w_ref[...].astype(BF16)) + b_ref[...]


def _modulation(c, w, b):
    batch, d = c.shape
    n = w.shape[1]
    return pl.pallas_call(
        _mod_kernel,
        grid=(n // MOD_BLOCK,),
        in_specs=[
            pl.BlockSpec((batch, d), lambda j: (0, 0)),
            pl.BlockSpec((d, MOD_BLOCK), lambda j: (0, j)),
            pl.BlockSpec((1, MOD_BLOCK), lambda j: (0, j)),
        ],
        out_specs=pl.BlockSpec((batch, MOD_BLOCK), lambda j: (0, j)),
        out_shape=jax.ShapeDtypeStruct((batch, n), F32),
        compiler_params=pltpu.CompilerParams(dimension_semantics=("arbitrary",)),
    )(c, w, b.reshape(1, n))


def _ffn_kernel(x_ref, mod_ref, nw_ref, w13_hbm, w2_hbm, *rest, mod_base, final_norm):
    o_ref, w13_ref, w2_ref, s_ref, stage13_ref, stage2_ref, sem_ref = rest[-7:]

    @pl.when(_is_first_step())
    def _():
        _load_weight_bf16(w13_hbm, w13_ref, stage13_ref, sem_ref)
        _load_weight_bf16(w2_hbm, w2_ref, stage2_ref, sem_ref)

    x = x_ref[...]
    shift, scale, gate = _mod_rows(mod_ref, mod_base)
    hb = _rms_mod(x, nw_ref[...], shift, scale).astype(BF16)

    def up(lhs, lo):
        return (_dot(lhs, w13_ref[:, lo:lo + FFN_BLOCK]),
                _dot(lhs, w13_ref[:, D_FF + lo:D_FF + lo + FFN_BLOCK]))

    for lo in range(0, D_FF, FFN_BLOCK):
        if lo == 0:
            parts = [up(hb[r:r + HEAD_ROWS], lo) for r in range(0, x.shape[0], HEAD_ROWS)]
            a = jnp.concatenate([p[0] for p in parts], axis=0)
            b = jnp.concatenate([p[1] for p in parts], axis=0)
        else:
            a, b = up(hb, lo)
        s_ref[:, lo:lo + FFN_BLOCK] = (_silu(a) * b).astype(BF16)
    for r in range(0, x.shape[0], TAIL_ROWS):
        rows = slice(r, r + TAIL_ROWS)
        out = x[rows] + gate * (0.5 * _dot(s_ref[rows, :], w2_ref[...]))
        if final_norm:
            ms = jnp.mean(out * out, axis=-1, keepdims=True)
            out = out * lax.rsqrt(ms + EPS) * rest[0][...]
        o_ref[rows, :] = out


def _ffn(x, mod, mod_base, norm_w, w13, w2, final_w):
    batch, seq, d = x.shape
    tm = FFN_TOKENS
    final_norm = final_w is not None
    in_specs = [
        pl.BlockSpec((None, tm, d), lambda b, i: (b, i, 0)),
        _resident(mod.shape),
        _resident((1, d)),
        pl.BlockSpec(memory_space=pl.ANY),
        pl.BlockSpec(memory_space=pl.ANY),
    ]
    args = [x, mod, norm_w.reshape(1, d), w13, w2]
    if final_norm:
        in_specs.append(_resident((1, d)))
        args.append(final_w.reshape(1, d))
    return pl.pallas_call(
        functools.partial(_ffn_kernel, mod_base=mod_base, final_norm=final_norm),
        grid=(batch, seq // tm),
        in_specs=in_specs,
        out_specs=pl.BlockSpec((None, tm, d), lambda b, i: (b, i, 0)),
        out_shape=jax.ShapeDtypeStruct(x.shape, F32),
        scratch_shapes=[
            pltpu.VMEM(w13.shape, BF16),
            pltpu.VMEM(w2.shape, BF16),
            pltpu.VMEM((tm, D_FF), BF16),
            pltpu.VMEM((STAGE_SLOTS, FFN_STAGE_ROWS[0], w13.shape[1]), F32),
            pltpu.VMEM((STAGE_SLOTS, FFN_STAGE_ROWS[1], w2.shape[1]), F32),
            pltpu.SemaphoreType.DMA((STAGE_SLOTS,)),
        ],
        compiler_params=pltpu.CompilerParams(
            dimension_semantics=("arbitrary", "arbitrary"),
            vmem_limit_bytes=V7X_VMEM_LIMIT_BYTES),
    )(*args)


def _rope(t, cos, sin):
    t1, t2 = t[:, :ROPE_HALF], t[:, ROPE_HALF:]
    return jnp.concatenate([t1 * cos - t2 * sin, t1 * sin + t2 * cos], axis=-1)


def _mixer_kernel(x_ref, mod_ref, nw_ref, cos_ref, sin_ref, idec_ref, qdec_ref, kdec_ref,
                  cdec_ref, w_in_hbm, gnw_ref, w_r_hbm, plin_hbm, pscale_ref, w_p_hbm,
                  w_out_hbm, o_ref, state_ref, uext_ref, gated_ref,
                  w_in_ref, w_r_ref, plin_ref, w_p_ref, w_out_ref,
                  stage_ref, stage_plin_ref, sem_ref):
    i = pl.program_id(1)
    tm = x_ref.shape[0]
    n_chunks = tm // RET_CHUNK

    @pl.when(_is_first_step())
    def _():
        _load_weight_bf16(w_in_hbm, w_in_ref, stage_ref, sem_ref)
        _load_weight_bf16(w_r_hbm, w_r_ref, stage_ref, sem_ref)
        _load_weight_bf16(w_p_hbm, w_p_ref, stage_ref, sem_ref)
        _load_weight_bf16(w_out_hbm, w_out_ref, stage_ref, sem_ref)
        _load_weight_bf16(plin_hbm, plin_ref, stage_plin_ref, sem_ref)
        for gi in range(len(POOL_WINDOWS)):
            cols = slice(gi * POOL_GROUP_W, (gi + 1) * POOL_GROUP_W)
            lin = plin_ref[cols, :].astype(F32) * pscale_ref[:, cols]
            hi = lin.astype(BF16)
            lo = (lin - hi.astype(F32)).astype(BF16)
            w_g = w_p_ref[cols, :]
            w_p_ref[cols, :] = (_dot(hi, w_g) + _dot(lo, w_g)).astype(BF16)

    @pl.when(i == 0)
    def _():
        state_ref[...] = jnp.zeros(state_ref.shape, F32)
        uext_ref[0:POOL_HALO, :] = jnp.zeros((POOL_HALO, POOL_W), F32)

    x = x_ref[...]
    shift, scale, gate = _mod_rows(mod_ref, 3)
    hb = _rms_mod(x, nw_ref[...], shift, scale).astype(BF16)
    cos = cos_ref[...]
    sin = sin_ref[...]

    def proj(off, width):
        return _dot(hb, w_in_ref[:, off:off + width])

    qk, roped, vg, scores, start_state = {}, {}, {}, {}, {}

    def proj_qk(h):
        qk[h] = (proj(OFF_Q + h * RET_DK, RET_DK), proj(OFF_K + h * RET_DK, RET_DK))

    def chunk_rows(c):
        return slice(c * RET_CHUNK, (c + 1) * RET_CHUNK)

    def rope_qk(h):
        q, k = qk.pop(h)
        qr = _rope(q, cos, sin)
        kr = _rope(k, cos, sin) * (RET_DK ** -0.5)
        qdec = jnp.concatenate([qdec_ref[h]] * (RET_DK // ROPE_HALF), axis=1)
        kdec = jnp.concatenate([kdec_ref[h]] * (RET_DK // ROPE_HALF), axis=1)
        roped[h] = [(qr[chunk_rows(c)].astype(BF16), (qr[chunk_rows(c)] * qdec).astype(BF16),
                     kr[chunk_rows(c)].astype(BF16), (kr[chunk_rows(c)] * kdec).astype(BF16))
                    for c in range(n_chunks)]

    def proj_vg(h):
        v = proj(OFF_V + h * RET_DV, RET_DV)
        vg[h] = (v.astype(BF16), proj(OFF_G + h * RET_DV, RET_DV))

    def state_step(h, c):
        _, _, _, kd = roped[h][c]
        st = state_ref[h]
        start_state[h].append(st.astype(BF16))
        upd = lax.dot_general(kd, vg[h][0][chunk_rows(c)], (((0,), (0,)), ((), ())),
                              preferred_element_type=F32)
        state_ref[h] = cdec_ref[h] * st + upd

    def retention_first(h):
        scores[h] = [lax.dot_general(qb, kb, (((1,), (1,)), ((), ())),
                                     preferred_element_type=F32)
                     for qb, _, kb, _ in roped[h]]
        start_state[h] = []
        state_step(h, 0)

    def retention_rest(h):
        for c in range(1, n_chunks):
            state_step(h, c)

    def finish_head(h):
        vb, g = vg.pop(h)
        idec = idec_ref[h]
        outs = [_dot(jnp.concatenate([(s * idec).astype(BF16), qd], axis=1),
                     jnp.concatenate([vb[chunk_rows(c)], stb], axis=0))
                for c, (s, (_, qd, _, _), stb) in enumerate(
                    zip(scores.pop(h), roped.pop(h), start_state.pop(h)))]
        o = outs[0] if n_chunks == 1 else jnp.concatenate(outs, axis=0)
        mu = jnp.mean(o, axis=-1, keepdims=True)
        oc = o - mu
        var = jnp.mean(oc * oc, axis=-1, keepdims=True)
        ret = oc * lax.rsqrt(var + EPS) * gnw_ref[:, h * RET_DV:(h + 1) * RET_DV]
        gated_ref[:, h * RET_DV:(h + 1) * RET_DV] = (_silu(g) * ret).astype(BF16)

    def pool_windows(u):
        uext_ref[POOL_HALO:POOL_HALO + tm, :] = u
        pos = (i * tm + lax.broadcasted_iota(jnp.int32, (tm, 1), 0)).astype(F32)
        pooled = []
        for gi, w in enumerate(POOL_WINDOWS):
            cols = slice(gi * POOL_GROUP_W, (gi + 1) * POOL_GROUP_W)
            s = uext_ref[:, cols]
            span = 1
            while span < w:
                s = s + pltpu.roll(s, span, axis=0)
                span *= 2
            count = jnp.minimum(pos + 1.0, float(w))
            pooled.append((s[POOL_HALO:, :] / count - u[:, cols]).astype(BF16))
        uext_ref[0:POOL_HALO, :] = uext_ref[tm:tm + POOL_HALO, :]
        return jnp.concatenate(pooled, axis=1)

    proj_qk(0)
    u = proj(OFF_U, POOL_W)
    proj_vg(0)
    rope_qk(0)
    pooled = pool_windows(u)
    for h in range(RET_HEADS):
        last = h + 1 == RET_HEADS
        if not last:
            proj_qk(h + 1)
        retention_first(h)
        if not last:
            rope_qk(h + 1)
            proj_vg(h + 1)
        else:
            a_r = proj(OFF_AR, D_MODEL)
            a_p = proj(OFF_AP, D_MODEL)
        retention_rest(h)
        finish_head(h)
    y_p = _dot(pooled, w_p_ref[...])

    chunks = [chunk_rows(c) for c in range(n_chunks)]
    y_r = [_dot(gated_ref[rows, :], w_r_ref[...]) for rows in chunks[:1]]
    for c, rows in enumerate(chunks):
        if c + 1 < n_chunks:
            y_r.append(_dot(gated_ref[chunks[c + 1], :], w_r_ref[...]))
        merged = (jax.nn.sigmoid(a_r[rows]) * y_r[c]
                  + jax.nn.sigmoid(a_p[rows]) * y_p[rows])
        o_ref[rows, :] = x[rows] + gate * _dot(merged.astype(BF16), w_out_ref[...])


def _retention_tables():
    c = RET_CHUNK
    log_gamma = np.log1p(-(2.0 ** (-5.0 - np.arange(RET_HEADS, dtype=np.float32)))).astype(np.float32)
    idx = np.arange(c, dtype=np.float32)
    diff = idx[:, None] - idx[None, :]
    inner = np.where(diff >= 0, np.exp(log_gamma[:, None, None] * np.maximum(diff, 0.0)), 0.0)
    q_decay = np.exp(log_gamma[:, None] * (idx + 1.0))
    k_decay = np.exp(log_gamma[:, None] * (c - 1.0 - idx))
    chunk_decay = np.exp(log_gamma * c)
    wide = (RET_HEADS, c, ROPE_HALF)
    return (inner.astype(np.float32),
            np.ascontiguousarray(np.broadcast_to(q_decay[:, :, None], wide), dtype=np.float32),
            np.ascontiguousarray(np.broadcast_to(k_decay[:, :, None], wide), dtype=np.float32),
            chunk_decay.astype(np.float32))


def _rope_tables(seq):
    inv = (1.0 / (ROPE_BASE ** (np.arange(ROPE_HALF, dtype=np.float32) / ROPE_HALF))).astype(np.float32)
    ang = np.arange(seq, dtype=np.float32)[:, None] * inv[None, :]
    return np.cos(ang).astype(np.float32), np.sin(ang).astype(np.float32)


def _mixer(x, mod, norm_w, w_in, gn_w, w_r, pool_lin, pool_scale, w_p, w_out):
    batch, seq, d = x.shape
    tm = MIX_TOKENS
    cos, sin = _rope_tables(seq)
    idec, qdec, kdec, cdec = _retention_tables()
    in_specs = [
        pl.BlockSpec((None, tm, d), lambda b, i: (b, i, 0)),
        _resident(mod.shape),
        _resident((1, d)),
        pl.BlockSpec((tm, ROPE_HALF), lambda b, i: (i, 0)),
        pl.BlockSpec((tm, ROPE_HALF), lambda b, i: (i, 0)),
        _resident(idec.shape),
        _resident(qdec.shape),
        _resident(kdec.shape),
        pl.BlockSpec(memory_space=pltpu.SMEM),
        pl.BlockSpec(memory_space=pl.ANY),
        _resident((1, RET_HEADS * RET_DV)),
        pl.BlockSpec(memory_space=pl.ANY),
        pl.BlockSpec(memory_space=pl.ANY),
        _resident((1, POOL_W)),
        pl.BlockSpec(memory_space=pl.ANY),
        pl.BlockSpec(memory_space=pl.ANY),
    ]
    pool_lin = pool_lin.reshape(POOL_W, POOL_GROUP_W)
    return pl.pallas_call(
        _mixer_kernel,
        grid=(batch, seq // tm),
        in_specs=in_specs,
        out_specs=pl.BlockSpec((None, tm, d), lambda b, i: (b, i, 0)),
        out_shape=jax.ShapeDtypeStruct(x.shape, F32),
        scratch_shapes=[
            pltpu.VMEM((RET_HEADS, RET_DK, RET_DV), F32),
            pltpu.VMEM((POOL_HALO + tm, POOL_W), F32),
            pltpu.VMEM((tm, RET_HEADS * RET_DV), BF16),
            pltpu.VMEM(w_in.shape, BF16),
            pltpu.VMEM(w_r.shape, BF16),
            pltpu.VMEM(pool_lin.shape, BF16),
            pltpu.VMEM(w_p.shape, BF16),
            pltpu.VMEM(w_out.shape, BF16),
            pltpu.VMEM((MIX_STAGE_SLOTS, MIX_STAGE_ROWS, d), F32),
            pltpu.VMEM((STAGE_SLOTS, POOL_STAGE_ROWS, POOL_GROUP_W), F32),
            pltpu.SemaphoreType.DMA((MIX_STAGE_SLOTS,)),
        ],
        compiler_params=pltpu.CompilerParams(
            dimension_semantics=("arbitrary", "arbitrary"),
            vmem_limit_bytes=V7X_VMEM_LIMIT_BYTES),
    )(x, mod, norm_w.reshape(1, d), cos, sin, idec, qdec, kdec, cdec, w_in,
      gn_w.reshape(1, -1), w_r, pool_lin, pool_scale.reshape(1, -1), w_p, w_out)


def kernel(x, c, ada_w, ada_b, norm_ffn1, ffn1_w13, ffn1_w2, norm_mix, w_in, ret_gn_w,
           w_ret_branch, pool_lin, pool_scale, w_pool_branch, w_out, norm_ffn2, ffn2_w13,
           ffn2_w2, norm_final):
    depth = ada_w.shape[0]
    assert x.shape[1] % FFN_TOKENS == 0 and x.shape[1] % MIX_TOKENS == 0
    assert x.shape[2] == D_MODEL and ada_w.shape[2] == N_MOD * D_MODEL
    for l in range(depth):
        mod = _modulation(c, ada_w[l], ada_b[l])
        x = _ffn(x, mod, 0, norm_ffn1[l], ffn1_w13[l], ffn1_w2[l], None)
        x = _mixer(x, mod, norm_mix[l], w_in[l], ret_gn_w[l], w_ret_branch[l], pool_lin[l],
                   pool_scale[l], w_pool_branch[l], w_out[l])
        x = _ffn(x, mod, 6, norm_ffn2[l], ffn2_w13[l], ffn2_w2[l],
                 norm_final if l == depth - 1 else None)
    return x
```
